```python
import jax, jax.numpy as jnp
from jax import lax
import numpy as np

D_MODEL = 2048
BATCH = 4
SEQ = 2048
DEPTH = 4

PLE_DIM = 256
D_CONV = D_MODEL // 2
CONV_WIDTH = 31
HG_DK = 128
HG_DV = 128
HG_HEADS = (D_MODEL - D_CONV) // HG_DV
D_HG = HG_HEADS * HG_DV
D_MIX = D_CONV + D_HG
CHUNK = 64
D_FF = 5632
N_EXPERTS = 8
TOP_K = 2
N_DENSE = (DEPTH + 1) // 2
N_MOE = DEPTH // 2
RMS_EPS = 1e-6
LN_EPS = 1e-5
HG_K_TOTAL = HG_HEADS * HG_DK
SPLITS = (D_CONV, 2 * D_CONV, 2 * D_CONV + HG_K_TOTAL, 2 * D_CONV + 2 * HG_K_TOTAL,
          2 * D_CONV + 2 * HG_K_TOTAL + D_HG)
D_IN = 2 * D_CONV + 2 * HG_K_TOTAL + 2 * D_HG

kernel_name = 'hymba_conv_hgrn2_moe_ple_trunk'


def rmsnorm(x, g):
    xf = x.astype(jnp.float32)
    y = xf * lax.rsqrt(jnp.mean(xf * xf, axis=-1, keepdims=True) + RMS_EPS)
    return (y * g.astype(jnp.float32)).astype(x.dtype)


def conformer_conv(val, gate, w_dw, b_dw, ln_g, ln_b):
    f32 = jnp.float32
    u = val.astype(f32) * jax.nn.sigmoid(gate.astype(f32))
    c = lax.conv_general_dilated(
        u, w_dw.astype(f32)[:, None, :], window_strides=(1,),
        padding=[(CONV_WIDTH - 1, 0)], dimension_numbers=('NWC', 'WIO', 'NWC'),
        feature_group_count=D_CONV)
    c = c + b_dw.astype(f32)
    mu = jnp.mean(c, axis=-1, keepdims=True)
    var = jnp.mean(jnp.square(c - mu), axis=-1, keepdims=True)
    c = (c - mu) * lax.rsqrt(var + LN_EPS) * ln_g.astype(f32) + ln_b.astype(f32)
    return jax.nn.silu(c)


def hgrn2(q, f_logit, i, g, lb, gn):
    f32 = jnp.float32
    B, S, _ = q.shape
    n_chunks = S // CHUNK
    f = lb + (1.0 - lb) * jax.nn.sigmoid(f_logit.astype(f32))
    logf = jnp.log(f)
    k = 1.0 - f

    def heads(t, d):
        return t.astype(f32).reshape(B, n_chunks, CHUNK, HG_HEADS, d).transpose(1, 0, 3, 2, 4)

    qs, ks, lfs, vs = heads(q, HG_DK), heads(k, HG_DK), heads(logf, HG_DK), heads(i, HG_DV)
    causal = jnp.tril(jnp.ones((CHUNK, CHUNK), dtype=bool))[:, :, None]

    def step(state, xs):
        qc, kc, vc, lfc = xs
        b = jnp.cumsum(lfc, axis=2)
        diff = b[:, :, :, None, :] - b[:, :, None, :, :]
        decay = jnp.exp(jnp.where(causal, diff, -jnp.inf))
        scores = jnp.einsum('bhtsk,bhsk->bhts', decay * qc[:, :, :, None, :], kc)
        o = (jnp.einsum('bhtk,bhkv->bhtv', qc * jnp.exp(b), state)
             + jnp.einsum('bhts,bhsv->bhtv', scores, vc))
        b_last = b[:, :, -1, :]
        k_dec = kc * jnp.exp(b_last[:, :, None, :] - b)
        new_state = (jnp.exp(b_last)[..., None] * state
                     + jnp.einsum('bhsk,bhsv->bhkv', k_dec, vc))
        return new_state, o

    s0 = jnp.zeros((B, HG_HEADS, HG_DK, HG_DV), f32)
    _, o = lax.scan(step, s0, (qs, ks, vs, lfs))
    o = o.transpose(1, 0, 3, 2, 4).reshape(B, S, HG_HEADS, HG_DV)
    o = o * lax.rsqrt(jnp.mean(o * o, axis=-1, keepdims=True) + RMS_EPS) * gn.astype(f32)
    return o.reshape(B, S, D_HG) * jax.nn.silu(g.astype(f32))


def swiglu(x, w_gu, w_dn):
    a, b = jnp.split(x @ w_gu, 2, axis=-1)
    return (jax.nn.silu(a) * b) @ w_dn


def moe_swiglu(x, w_router, b_router, w_gu, w_dn):
    f32 = jnp.float32
    logits = (x @ w_router).astype(f32) + b_router.astype(f32)
    top_v, top_i = lax.top_k(logits, TOP_K)
    top_w = jax.nn.softmax(top_v, axis=-1)
    gates = jnp.sum(jax.nn.one_hot(top_i, N_EXPERTS, dtype=f32) * top_w[..., None], axis=-2)
    out = jnp.zeros(x.shape, f32)
    for e in range(N_EXPERTS):
        out = out + gates[..., e:e + 1] * swiglu(x, w_gu[e], w_dn[e]).astype(f32)
    return out.astype(x.dtype)


def setup_inputs(seed: int = 0) -> dict:
    key = jax.random.key(seed)
    ks = jax.random.split(key, 24)
    f32 = jnp.float32
    nrm = jax.random.normal

    def w(k, shape, fan_in):
        return nrm(k, shape, f32) * fan_in ** -0.5

    def gain(k, shape):
        return 1.0 + 0.02 * nrm(k, shape, f32)

    return {
        'x': nrm(ks[0], (BATCH, SEQ, D_MODEL), f32),
        'p': nrm(ks[1], (DEPTH, BATCH, SEQ, PLE_DIM), f32),
        'mix_norm': gain(ks[2], (DEPTH, D_MODEL)),
        'w_in': w(ks[3], (DEPTH, D_MODEL, D_IN), D_MODEL),
        'conv_w': w(ks[4], (DEPTH, CONV_WIDTH, D_CONV), CONV_WIDTH),
        'conv_b': 0.02 * nrm(ks[5], (DEPTH, D_CONV), f32),
        'conv_ln_g': gain(ks[6], (DEPTH, D_CONV)),
        'conv_ln_b': 0.02 * nrm(ks[7], (DEPTH, D_CONV), f32),
        'hg_lb_logits': nrm(ks[8], (DEPTH, HG_K_TOTAL), f32),
        'hg_norm': gain(ks[9], (DEPTH, HG_DV)),
        'w_out': w(ks[10], (DEPTH, D_MIX, D_MODEL), D_MIX),
        'ffn_norm': gain(ks[11], (DEPTH, D_MODEL)),
        'ffn_w_gu': w(ks[12], (N_DENSE, D_MODEL, 2 * D_FF), D_MODEL),
        'ffn_w_dn': w(ks[13], (N_DENSE, D_FF, D_MODEL), D_FF),
        'router_w': w(ks[14], (N_MOE, D_MODEL, N_EXPERTS), D_MODEL),
        'router_b': 0.01 * nrm(ks[15], (N_MOE, N_EXPERTS), f32),
        'moe_w_gu': w(ks[16], (N_MOE, N_EXPERTS, D_MODEL, 2 * D_FF), D_MODEL),
        'moe_w_dn': w(ks[17], (N_MOE, N_EXPERTS, D_FF, D_MODEL), D_FF),
        'ple_w': w(ks[18], (DEPTH, PLE_DIM, D_MODEL), PLE_DIM),
        'ple_norm': gain(ks[19], (DEPTH, D_MODEL)),
        'ple_gate_w': w(ks[20], (DEPTH, D_MODEL, D_MODEL), D_MODEL),
        'final_norm': gain(ks[21], (D_MODEL,)),
    }


def reference(x, p, mix_norm, w_in, conv_w, conv_b, conv_ln_g, conv_ln_b, hg_lb_logits,
              hg_norm, w_out, ffn_norm, ffn_w_gu, ffn_w_dn, router_w, router_b,
              moe_w_gu, moe_w_dn, ple_w, ple_norm, ple_gate_w, final_norm):
    lb_all = jnp.cumsum(jax.nn.softmax(hg_lb_logits.astype(jnp.float32), axis=0), axis=0)
    lb_all = lb_all - lb_all[:1]
    h = x
    for l in range(DEPTH):
        z = rmsnorm(h, mix_norm[l]) @ w_in[l]
        c_val, c_gate, hq, hf, hi, hg = jnp.split(z, SPLITS, axis=-1)
        y_conv = conformer_conv(c_val, c_gate, conv_w[l], conv_b[l], conv_ln_g[l], conv_ln_b[l])
        y_hg = hgrn2(hq, hf, hi, hg, lb_all[l], hg_norm[l])
        y = jnp.concatenate([y_conv, y_hg], axis=-1).astype(h.dtype) @ w_out[l]
        h = h + y
        hn = rmsnorm(h, ffn_norm[l])
        if l % 2 == 0:
            h = h + swiglu(hn, ffn_w_gu[l // 2], ffn_w_dn[l // 2])
        else:
            h = h + moe_swiglu(hn, router_w[l // 2], router_b[l // 2],
                               moe_w_gu[l // 2], moe_w_dn[l // 2])
        e = p[l].astype(h.dtype) @ ple_w[l]
        gate = jax.nn.sigmoid((rmsnorm(h, ple_norm[l]) @ ple_gate_w[l]).astype(jnp.float32))
        h = h + (e.astype(jnp.float32) * gate).astype(h.dtype)
    return rmsnorm(h, final_norm)
```

```python
import functools

import jax
import jax.numpy as jnp
from jax import lax
from jax.experimental import pallas as pl
from jax.experimental.pallas import tpu as pltpu

F32 = jnp.float32
BF16 = jnp.bfloat16

RMS_EPS = 1e-6
LN_EPS = 1e-5
CONV_WIDTH = 31
HG_DK = 128
HG_DV = 128
CHUNK = 64
N_EXPERTS = 8
TOP_K = 2

LANES = 128
VMEM_LIMIT = 56 * 1024 * 1024

CONV_HALO = 32
CONV_STRIP = 32


def _params(*sem):
    return pltpu.CompilerParams(dimension_semantics=sem, vmem_limit_bytes=VMEM_LIMIT)


def _rmsnorm_body(x_ref, g_ref, o_ref):
    x = x_ref[...]
    ms = jnp.mean(x * x, axis=-1, keepdims=True)
    o_ref[...] = (x * lax.rsqrt(ms + RMS_EPS) * g_ref[...]).astype(o_ref.dtype)


def rmsnorm(h, gains, layer, out_dtype, bm=512):
    t, d = h.shape
    return pl.pallas_call(
        _rmsnorm_body,
        grid=(t // bm,),
        in_specs=[pl.BlockSpec((bm, d), lambda i: (i, 0)),
                  pl.BlockSpec((None, 1, d), lambda i: (layer, 0, 0))],
        out_specs=pl.BlockSpec((bm, d), lambda i: (i, 0)),
        out_shape=jax.ShapeDtypeStruct((t, d), out_dtype),
        compiler_params=_params("parallel"),
        name="rmsnorm",
    )(h, gains)


def _proj_in_body(x_ref, w_ref, o_ref, wbf_ref):
    @pl.when(pl.program_id(1) == 0)
    def _():
        wbf_ref[...] = w_ref[...].astype(BF16)

    o_ref[...] = jnp.dot(x_ref[...], wbf_ref[...], preferred_element_type=F32)


def proj_in(xn, w_in, layer, bm=1024, bn=1024):
    t, d = xn.shape
    n = w_in.shape[-1]
    return pl.pallas_call(
        _proj_in_body,
        grid=(n // bn, t // bm),
        in_specs=[pl.BlockSpec((bm, d), lambda j, i: (i, 0)),
                  pl.BlockSpec((None, d, bn), lambda j, i: (layer, 0, j))],
        out_specs=pl.BlockSpec((bm, bn), lambda j, i: (i, j)),
        out_shape=jax.ShapeDtypeStruct((t, n), F32),
        scratch_shapes=[pltpu.VMEM((d, bn), BF16)],
        compiler_params=_params("arbitrary", "arbitrary"),
        name="proj_in",
    )(xn, w_in)


def _conv_body(val_ref, gate_ref, w_ref, b_ref, lng_ref, lnb_ref, o_ref, u_ref, *, ts):
    s = pl.program_id(1)

    @pl.when(s == 0)
    def _():
        u_ref[0:CONV_HALO, :] = jnp.zeros((CONV_HALO, u_ref.shape[1]), F32)
        u_ref[CONV_HALO + ts:, :] = jnp.zeros((8, u_ref.shape[1]), F32)

    @pl.when(s > 0)
    def _():
        u_ref[0:CONV_HALO, :] = u_ref[ts:ts + CONV_HALO, :]

    u_ref[CONV_HALO:CONV_HALO + ts, :] = val_ref[...] * jax.nn.sigmoid(gate_ref[...])

    bias = b_ref[...]
    lng = lng_ref[...]
    lnb = lnb_ref[...]
    first_tap = CONV_HALO - (CONV_WIDTH - 1)

    def strip(r, carry):
        r0 = pl.multiple_of(r * CONV_STRIP, CONV_STRIP)
        win = u_ref[pl.ds(r0, CONV_STRIP + CONV_HALO + 8), :]
        acc = jnp.zeros((CONV_STRIP, u_ref.shape[1]), F32)
        for sh in range(8):
            shifted = win[sh:sh + CONV_STRIP + CONV_HALO, :]
            for a in range((CONV_HALO + 8) // 8):
                j = 8 * a + sh - first_tap
                if 0 <= j < CONV_WIDTH:
                    acc = acc + w_ref[j:j + 1, :] * shifted[8 * a:8 * a + CONV_STRIP, :]
        c = acc + bias
        mu = jnp.mean(c, axis=-1, keepdims=True)
        cc = c - mu
        var = jnp.mean(cc * cc, axis=-1, keepdims=True)
        y = cc * lax.rsqrt(var + LN_EPS) * lng + lnb
        o_ref[pl.ds(r0, CONV_STRIP), :] = (y * jax.nn.sigmoid(y)).astype(o_ref.dtype)
        return carry

    lax.fori_loop(0, ts // CONV_STRIP, strip, 0)


def conv_group(z, conv_w, conv_b, ln_g, ln_b, layer, batch, seq, d_conv, ts=256):
    t = z.shape[0]
    nst = seq // ts
    vec = lambda: pl.BlockSpec((None, 1, d_conv), lambda b, s: (layer, 0, 0))
    return pl.pallas_call(
        functools.partial(_conv_body, ts=ts),
        grid=(batch, nst),
        in_specs=[pl.BlockSpec((ts, d_conv), lambda b, s: (b * nst + s, 0)),
                  pl.BlockSpec((ts, d_conv), lambda b, s: (b * nst + s, 1)),
                  pl.BlockSpec((None, CONV_WIDTH, d_conv), lambda b, s: (layer, 0, 0)),
                  vec(), vec(), vec()],
        out_specs=pl.BlockSpec((ts, d_conv), lambda b, s: (b * nst + s, 0)),
        out_shape=jax.ShapeDtypeStruct((t, d_conv), BF16),
        scratch_shapes=[pltpu.VMEM((CONV_HALO + ts + 8, d_conv), F32)],
        compiler_params=_params("arbitrary", "arbitrary"),
        name="conv_group",
    )(z, z, conv_w, conv_b, ln_g, ln_b)


_HALF_BLOCKS = (32, 16, 8, 4, 2, 1)


def _hgrn_body(lbl_ref, gn_ref, q_ref, f_ref, i_ref, g_ref, o_ref, *, layer, n_chunks):
    lg = lbl_ref[...]
    ex = jnp.exp(lg - jnp.max(lg, axis=0, keepdims=True))
    sm = ex / jnp.sum(ex, axis=0, keepdims=True)
    lb = jnp.zeros((1, HG_DK), F32)
    for l in range(1, layer + 1):
        lb = lb + sm[l:l + 1, :]
    gn = gn_ref[...]

    c = CHUNK
    row = lax.broadcasted_iota(jnp.int32, (c, c), 0)
    col = lax.broadcasted_iota(jnp.int32, (c, c), 1)
    sel = [col <= row]
    masks = []
    for hb in _HALF_BLOCKS:
        ref_row = (row // (2 * hb)) * (2 * hb) + (hb - 1)
        sel.append(col <= ref_row)
        masks.append((row // (2 * hb) == col // (2 * hb))
                     & (row % (2 * hb) >= hb) & (col % (2 * hb) < hb))
    sel = jnp.concatenate(sel, axis=0).astype(F32)
    eye = row == col
    nt = (((1,), (1,)), ((), ()))
    tn = (((0,), (0,)), ((), ()))

    def chunk(ci, state_t):
        r0 = pl.multiple_of(ci * c, c)
        q = q_ref[pl.ds(r0, c), :]
        f = lb + (1.0 - lb) * jax.nn.sigmoid(f_ref[pl.ds(r0, c), :])
        logf = jnp.log(f)
        k = 1.0 - f
        v = i_ref[pl.ds(r0, c), :].astype(BF16)
        bs = jnp.dot(sel, logf, precision=lax.Precision.HIGHEST, preferred_element_type=F32)
        b = bs[0:c]
        scores = jnp.where(eye, lax.dot_general(q.astype(BF16), k.astype(BF16), nt,
                                                preferred_element_type=F32), 0.0)
        for lv in range(len(_HALF_BLOCKS)):
            e = jnp.exp(-jnp.abs(b - bs[(lv + 1) * c:(lv + 2) * c]))
            s_lv = lax.dot_general((q * e).astype(BF16), (k * e).astype(BF16), nt,
                                   preferred_element_type=F32)
            scores = scores + jnp.where(masks[lv], s_lv, 0.0)
        o = (lax.dot_general((q * jnp.exp(b)).astype(BF16), state_t.astype(BF16), nt,
                             preferred_element_type=F32)
             + jnp.dot(scores.astype(BF16), v, preferred_element_type=F32))
        b_last = b[c - 1:c, :]
        k_dec = (k * jnp.exp(b_last - b)).astype(BF16)
        new_state_t = jnp.exp(b_last) * state_t + lax.dot_general(
            v, k_dec, tn, preferred_element_type=F32)
        o = o * lax.rsqrt(jnp.mean(o * o, axis=-1, keepdims=True) + RMS_EPS) * gn
        g = g_ref[pl.ds(r0, c), :]
        o_ref[pl.ds(r0, c), :] = (o * (g * jax.nn.sigmoid(g))).astype(o_ref.dtype)
        return new_state_t

    lax.fori_loop(0, n_chunks, chunk, jnp.zeros((HG_DV, HG_DK), F32))


def hgrn_group(z, lb_logits, hg_norm, layer, batch, seq, heads, col0):
    t = z.shape[0]
    depth = lb_logits.shape[0]
    c0 = col0 // LANES
    zspec = lambda off: pl.BlockSpec((seq, LANES), lambda b, h: (b, c0 + off * heads + h))
    return pl.pallas_call(
        functools.partial(_hgrn_body, layer=layer, n_chunks=seq // CHUNK),
        grid=(batch, heads),
        in_specs=[pl.BlockSpec((depth, HG_DK), lambda b, h: (0, h)),
                  pl.BlockSpec((None, 1, HG_DV), lambda b, h: (layer, 0, 0)),
                  zspec(0), zspec(1), zspec(2), zspec(3)],
        out_specs=pl.BlockSpec((seq, HG_DV), lambda b, h: (b, h)),
        out_shape=jax.ShapeDtypeStruct((t, heads * HG_DV), BF16),
        compiler_params=_params("parallel", "parallel"),
        name="hgrn_group",
    )(lb_logits, hg_norm, z, z, z, z)


def _proj_out_body(yc_ref, yh_ref, w_ref, h_ref, o_ref, wbf_ref, *, d_conv):
    @pl.when(pl.program_id(1) == 0)
    def _():
        wbf_ref[...] = w_ref[...].astype(BF16)

    y = (jnp.dot(yc_ref[...], wbf_ref[0:d_conv, :], preferred_element_type=F32)
         + jnp.dot(yh_ref[...], wbf_ref[d_conv:, :], preferred_element_type=F32))
    o_ref[...] = h_ref[...] + y


def proj_out(yc, yh, w_out, h, layer, bm=1024, bn=512):
    t, d_conv = yc.shape
    d_hg = yh.shape[1]
    d = h.shape[1]
    return pl.pallas_call(
        functools.partial(_proj_out_body, d_conv=d_conv),
        grid=(d // bn, t // bm),
        in_specs=[pl.BlockSpec((bm, d_conv), lambda j, i: (i, 0)),
                  pl.BlockSpec((bm, d_hg), lambda j, i: (i, 0)),
                  pl.BlockSpec((None, d_conv + d_hg, bn), lambda j, i: (layer, 0, j)),
                  pl.BlockSpec((bm, bn), lambda j, i: (i, j))],
        out_specs=pl.BlockSpec((bm, bn), lambda j, i: (i, j)),
        out_shape=jax.ShapeDtypeStruct((t, d), F32),
        scratch_shapes=[pltpu.VMEM((d_conv + d_hg, bn), BF16)],
        compiler_params=_params("arbitrary", "arbitrary"),
        name="proj_out",
    )(yc, yh, w_out, h)


def _ffn_gu_body(te_ref, tf_ref, nu_ref, x_ref, wg_ref, wu_ref, o_ref, wgbf_ref, wubf_ref):
    i = pl.program_id(1)

    @pl.when(i < nu_ref[0])
    def _():
        @pl.when(tf_ref[i] == 1)
        def _():
            wgbf_ref[...] = wg_ref[...].astype(BF16)
            wubf_ref[...] = wu_ref[...].astype(BF16)

        x = x_ref[...].astype(BF16)
        a = jnp.dot(x, wgbf_ref[...], preferred_element_type=F32)
        b = jnp.dot(x, wubf_ref[...], preferred_element_type=F32)
        o_ref[...] = (a * jax.nn.sigmoid(a) * b).astype(o_ref.dtype)

    @pl.when(i >= nu_ref[0])
    def _():
        o_ref[...] = jnp.zeros(o_ref.shape, o_ref.dtype)


def ffn_gate_up(x, w_gu, tile_expert, tile_first, n_used, bm, bn=512):
    r, d = x.shape
    f = w_gu.shape[-1] // 2
    nfb = f // bn
    row = lambda j, i, te, tf, nu: (jnp.minimum(i, nu[0] - 1), 0)
    grid_spec = pltpu.PrefetchScalarGridSpec(
        num_scalar_prefetch=3,
        grid=(nfb, r // bm),
        in_specs=[pl.BlockSpec((bm, d), row),
                  pl.BlockSpec((None, d, bn), lambda j, i, te, tf, nu: (te[i], 0, j)),
                  pl.BlockSpec((None, d, bn), lambda j, i, te, tf, nu: (te[i], 0, nfb + j))],
        out_specs=pl.BlockSpec((bm, bn), lambda j, i, te, tf, nu: (i, j)),
        scratch_shapes=[pltpu.VMEM((d, bn), BF16), pltpu.VMEM((d, bn), BF16)],
    )
    return pl.pallas_call(
        _ffn_gu_body,
        grid_spec=grid_spec,
        out_shape=jax.ShapeDtypeStruct((r, f), BF16),
        compiler_params=_params("arbitrary", "arbitrary"),
        name="ffn_gate_up",
    )(tile_expert, tile_first, n_used, x, w_gu, w_gu)


def _ffn_dn_body(te_ref, tf_ref, nu_ref, x_ref, w_ref, *rest, residual):
    if residual:
        h_ref, o_ref, wbf_ref = rest
    else:
        o_ref, wbf_ref = rest
    i = pl.program_id(1)

    @pl.when(i < nu_ref[0])
    def _():
        @pl.when(tf_ref[i] == 1)
        def _():
            wbf_ref[...] = w_ref[...].astype(BF16)

        y = jnp.dot(x_ref[...], wbf_ref[...], preferred_element_type=F32)
        if residual:
            y = h_ref[...] + y
        o_ref[...] = y

    @pl.when(i >= nu_ref[0])
    def _():
        o_ref[...] = jnp.zeros(o_ref.shape, o_ref.dtype)


def ffn_down(x, w_dn, tile_expert, tile_first, n_used, bm, h=None, bn=512):
    r, f = x.shape
    d = w_dn.shape[-1]
    row = lambda j, i, te, tf, nu: (jnp.minimum(i, nu[0] - 1), 0)
    tile = lambda j, i, te, tf, nu: (i, j)
    in_specs = [pl.BlockSpec((bm, f), row),
                pl.BlockSpec((None, f, bn), lambda j, i, te, tf, nu: (te[i], 0, j))]
    args = [x, w_dn]
    if h is not None:
        in_specs.append(pl.BlockSpec((bm, bn), tile))
        args.append(h)
    grid_spec = pltpu.PrefetchScalarGridSpec(
        num_scalar_prefetch=3,
        grid=(d // bn, r // bm),
        in_specs=in_specs,
        out_specs=pl.BlockSpec((bm, bn), tile),
        scratch_shapes=[pltpu.VMEM((f, bn), BF16)],
    )
    return pl.pallas_call(
        functools.partial(_ffn_dn_body, residual=h is not None),
        grid_spec=grid_spec,
        out_shape=jax.ShapeDtypeStruct((r, d), F32),
        compiler_params=_params("arbitrary", "arbitrary"),
        name="ffn_down",
    )(tile_expert, tile_first, n_used, *args)


def _router_body(x_ref, w_ref, b_ref, meta_ref, cnt_ref, carry_ref, *, tm):
    @pl.when(pl.program_id(0) == 0)
    def _():
        carry_ref[...] = jnp.zeros(carry_ref.shape, F32)

    lane = lax.broadcasted_iota(jnp.int32, (tm, LANES), 1)
    logits = jnp.dot(x_ref[...], w_ref[...], precision=lax.Precision.HIGHEST,
                     preferred_element_type=F32) + b_ref[...]
    neg = jnp.float32(-jnp.inf)
    logits = jnp.where(lane < N_EXPERTS, logits, neg)
    m1 = jnp.max(logits, axis=-1, keepdims=True)
    i1 = jnp.min(jnp.where(logits == m1, lane, LANES), axis=-1, keepdims=True)
    rest = jnp.where(lane == i1, neg, logits)
    m2 = jnp.max(rest, axis=-1, keepdims=True)
    i2 = jnp.min(jnp.where(rest == m2, lane, LANES), axis=-1, keepdims=True)
    e2 = jnp.exp(m2 - m1)
    g1 = 1.0 / (1.0 + e2)
    g2 = e2 / (1.0 + e2)
    hot1 = lane == i1
    hot2 = lane == i2
    hot = (hot1 | hot2).astype(BF16)
    row = lax.broadcasted_iota(jnp.int32, (tm, tm), 0)
    col = lax.broadcasted_iota(jnp.int32, (tm, tm), 1)
    before = (col < row).astype(BF16)
    carry = carry_ref[0:1, :]
    rank = jnp.dot(before, hot, preferred_element_type=F32) + carry
    r1 = jnp.sum(jnp.where(hot1, rank, 0.0), axis=-1, keepdims=True)
    r2 = jnp.sum(jnp.where(hot2, rank, 0.0), axis=-1, keepdims=True)
    total = carry + jnp.sum(hot.astype(F32), axis=0, keepdims=True)
    carry_ref[...] = jnp.broadcast_to(total, carry_ref.shape)
    cnt_ref[...] = jnp.broadcast_to(total, cnt_ref.shape)
    meta = jnp.where(lane == 0, i1.astype(F32), 0.0)
    meta = jnp.where(lane == 1, i2.astype(F32), meta)
    meta = jnp.where(lane == 2, g1, meta)
    meta = jnp.where(lane == 3, g2, meta)
    meta = jnp.where(lane == 4, r1, meta)
    meta = jnp.where(lane == 5, r2, meta)
    meta_ref[...] = meta


def router(xn, w_pad, b_pad, tm=512):
    t, d = xn.shape
    return pl.pallas_call(
        functools.partial(_router_body, tm=tm),
        grid=(t // tm,),
        in_specs=[pl.BlockSpec((tm, d), lambda i: (i, 0)),
                  pl.BlockSpec((d, LANES), lambda i: (0, 0)),
                  pl.BlockSpec((1, LANES), lambda i: (0, 0))],
        out_specs=[pl.BlockSpec((tm, LANES), lambda i: (i, 0)),
                   pl.BlockSpec((8, LANES), lambda i: (0, 0))],
        out_shape=[jax.ShapeDtypeStruct((t, LANES), F32),
                   jax.ShapeDtypeStruct((8, LANES), F32)],
        scratch_shapes=[pltpu.VMEM((8, LANES), F32)],
        compiler_params=_params("arbitrary"),
        name="router",
    )(xn, w_pad, b_pad)


def _dispatch_body(pos_ref, x_ref, init_ref, o_ref, sem, *, tm):
    del init_ref
    base = pl.program_id(0) * (TOP_K * tm)

    def row_copy(r, k):
        p = pos_ref[base + TOP_K * r + k]
        return pltpu.make_async_copy(x_ref.at[pl.ds(r, 1)], o_ref.at[pl.ds(p, 1)], sem)

    def start(r, carry):
        for k in range(TOP_K):
            row_copy(r, k).start()
        return carry

    def wait(r, carry):
        for k in range(TOP_K):
            row_copy(r, k).wait()
        return carry

    lax.fori_loop(0, tm, start, 0)
    lax.fori_loop(0, tm, wait, 0)


def dispatch(xn, pos, rows, tm=512):
    t, d = xn.shape
    grid_spec = pltpu.PrefetchScalarGridSpec(
        num_scalar_prefetch=1,
        grid=(t // tm,),
        in_specs=[pl.BlockSpec((tm, d), lambda i, pos: (i, 0)),
                  pl.BlockSpec(memory_space=pl.ANY)],
        out_specs=pl.BlockSpec(memory_space=pl.ANY),
        scratch_shapes=[pltpu.SemaphoreType.DMA(())],
    )
    return pl.pallas_call(
        functools.partial(_dispatch_body, tm=tm),
        grid_spec=grid_spec,
        out_shape=jax.ShapeDtypeStruct((rows, d), xn.dtype),
        input_output_aliases={2: 0},
        compiler_params=_params("arbitrary"),
        name="moe_dispatch",
    )(pos, xn, jnp.zeros((rows, d), xn.dtype))


def _combine_body(pos_ref, y_ref, meta_ref, h_ref, o_ref, buf_ref, sem, *, tm):
    base = pl.program_id(0) * (TOP_K * tm)

    def row_copy(r, k):
        p = pos_ref[base + TOP_K * r + k]
        return pltpu.make_async_copy(y_ref.at[pl.ds(p, 1)], buf_ref.at[k, pl.ds(r, 1)], sem)

    def start(r, carry):
        for k in range(TOP_K):
            row_copy(r, k).start()
        return carry

    def wait(r, carry):
        for k in range(TOP_K):
            row_copy(r, k).wait()
        return carry

    lax.fori_loop(0, tm, start, 0)
    lax.fori_loop(0, tm, wait, 0)
    meta = meta_ref[...]
    o_ref[...] = h_ref[...] + meta[:, 2:3] * buf_ref[0] + meta[:, 3:4] * buf_ref[1]


def combine(y, pos, meta, h, tm=256):
    t, d = h.shape
    grid_spec = pltpu.PrefetchScalarGridSpec(
        num_scalar_prefetch=1,
        grid=(t // tm,),
        in_specs=[pl.BlockSpec(memory_space=pl.ANY),
                  pl.BlockSpec((tm, LANES), lambda i, pos: (i, 0)),
                  pl.BlockSpec((tm, d), lambda i, pos: (i, 0))],
        out_specs=pl.BlockSpec((tm, d), lambda i, pos: (i, 0)),
        scratch_shapes=[pltpu.VMEM((TOP_K, tm, d), F32), pltpu.SemaphoreType.DMA(())],
    )
    return pl.pallas_call(
        functools.partial(_combine_body, tm=tm),
        grid_spec=grid_spec,
        out_shape=jax.ShapeDtypeStruct((t, d), F32),
        compiler_params=_params("arbitrary"),
        name="moe_combine",
    )(pos, y, meta, h)


def moe_ffn(h, xn, router_w, router_b, w_gu, w_dn, moe_layer, bm=512):
    t, d = h.shape
    e = N_EXPERTS
    w_pad = jnp.pad(router_w[moe_layer], ((0, 0), (0, LANES - e)))
    b_pad = jnp.pad(router_b[moe_layer], (0, LANES - e)).reshape(1, LANES)
    meta, counts = router(xn, w_pad, b_pad)
    counts = counts[0, :e].astype(jnp.int32)
    tiles = (counts + bm - 1) // bm
    tile_end = jnp.cumsum(tiles)
    group_start = (tile_end - tiles) * bm
    n_tiles = (TOP_K * t + e * (bm - 1)) // bm
    n_used = tile_end[-1:]
    tile_ids = jnp.arange(n_tiles, dtype=jnp.int32)
    tile_expert = jnp.sum(tile_ids[:, None] >= tile_end[None, :], axis=1).astype(jnp.int32)
    tile_expert = jnp.minimum(tile_expert, tile_expert[n_used[0] - 1])
    tile_first = jnp.concatenate([jnp.ones((1,), jnp.int32),
                                  (tile_expert[1:] != tile_expert[:-1]).astype(jnp.int32)])
    sel = meta[:, 0:TOP_K].astype(jnp.int32)
    rank = meta[:, 4:4 + TOP_K].astype(jnp.int32)
    pos = (group_start[sel] + rank).reshape(-1)
    x_sorted = dispatch(xn, pos, n_tiles * bm)
    te = tile_expert + moe_layer * e
    mid = ffn_gate_up(x_sorted, w_gu, te, tile_first, n_used, bm)
    y = ffn_down(mid, w_dn, te, tile_first, n_used, bm)
    return combine(y, pos, meta, h)


def dense_ffn(h, xn, w_gu, w_dn, dense_layer, bm=512):
    t = h.shape[0]
    n_tiles = t // bm
    te = jnp.full((n_tiles,), dense_layer, jnp.int32)
    tf = jnp.zeros((n_tiles,), jnp.int32).at[0].set(1)
    nu = jnp.full((1,), n_tiles, jnp.int32)
    mid = ffn_gate_up(xn, w_gu, te, tf, nu, bm)
    return ffn_down(mid, w_dn, te, tf, nu, bm, h=h)


def _ple_body(p_ref, xn_ref, wp_ref, wg_ref, h_ref, o_ref, wpbf_ref, wgbf_ref):
    @pl.when(pl.program_id(1) == 0)
    def _():
        wpbf_ref[...] = wp_ref[...].astype(BF16)
        wgbf_ref[...] = wg_ref[...].astype(BF16)

    e = jnp.dot(p_ref[...].astype(BF16), wpbf_ref[...], preferred_element_type=F32)
    gate = jax.nn.sigmoid(jnp.dot(xn_ref[...], wgbf_ref[...], preferred_element_type=F32))
    o_ref[...] = h_ref[...] + e * gate


def ple(h, xn, p, ple_w, gate_w, layer, bm=1024, bn=512):
    t, d = h.shape
    pd = p.shape[-1]
    return pl.pallas_call(
        _ple_body,
        grid=(d // bn, t // bm),
        in_specs=[pl.BlockSpec((None, bm, pd), lambda j, i: (layer, i, 0)),
                  pl.BlockSpec((bm, d), lambda j, i: (i, 0)),
                  pl.BlockSpec((None, pd, bn), lambda j, i: (layer, 0, j)),
                  pl.BlockSpec((None, d, bn), lambda j, i: (layer, 0, j)),
                  pl.BlockSpec((bm, bn), lambda j, i: (i, j))],
        out_specs=pl.BlockSpec((bm, bn), lambda j, i: (i, j)),
        out_shape=jax.ShapeDtypeStruct((t, d), F32),
        scratch_shapes=[pltpu.VMEM((pd, bn), BF16), pltpu.VMEM((d, bn), BF16)],
        compiler_params=_params("arbitrary", "arbitrary"),
        name="ple",
    )(p, xn, ple_w, gate_w, h)


def kernel(x, p, mix_norm, w_in, conv_w, conv_b, conv_ln_g, conv_ln_b, hg_lb_logits, hg_norm, w_out, ffn_norm, ffn_w_gu, ffn_w_dn, router_w, router_b, moe_w_gu, moe_w_dn, ple_w, ple_norm, ple_gate_w, final_norm):
    batch, seq, d = x.shape
    depth = w_in.shape[0]
    t = batch * seq
    d_conv = conv_w.shape[-1]
    heads = (w_out.shape[1] - d_conv) // HG_DV
    col_hg = 2 * d_conv

    h = x.reshape(t, d)
    p2 = p.reshape(depth, t, p.shape[-1])
    vec3 = lambda a: a.reshape(a.shape[0], 1, a.shape[-1])
    mix_g, ffn_g, ple_g = vec3(mix_norm), vec3(ffn_norm), vec3(ple_norm)
    cb, lng, lnb, gn = vec3(conv_b), vec3(conv_ln_g), vec3(conv_ln_b), vec3(hg_norm)
    moe_gu = moe_w_gu.reshape((-1,) + moe_w_gu.shape[2:])
    moe_dn = moe_w_dn.reshape((-1,) + moe_w_dn.shape[2:])

    for l in range(depth):
        z = proj_in(rmsnorm(h, mix_g, l, BF16), w_in, l)
        yc = conv_group(z, conv_w, cb, lng, lnb, l, batch, seq, d_conv)
        yh = hgrn_group(z, hg_lb_logits, gn, l, batch, seq, heads, col_hg)
        h = proj_out(yc, yh, w_out, h, l)
        if l % 2 == 0:
            h = dense_ffn(h, rmsnorm(h, ffn_g, l, BF16), ffn_w_gu, ffn_w_dn, l // 2)
        else:
            h = moe_ffn(h, rmsnorm(h, ffn_g, l, F32), router_w, router_b, moe_gu, moe_dn, l // 2)
        h = ple(h, rmsnorm(h, ple_g, l, BF16), p2, ple_w, ple_gate_w, l)
    out = rmsnorm(h, final_norm.reshape(1, 1, d), 0, F32)
    return out.reshape(batch, seq, d)
```

```python
import functools

import jax
import jax.numpy as jnp
from jax import lax
from jax.experimental import pallas as pl
from jax.experimental.pallas import tpu as pltpu

F32 = jnp.float32
BF16 = jnp.bfloat16

RMS_EPS = 1e-6
LN_EPS = 1e-5
CONV_WIDTH = 31
HG_DK = 128
HG_DV = 128
CHUNK = 64
N_EXPERTS = 8
TOP_K = 2

LANES = 128
SUBLANES = 8
HGRN_UNROLL = 8
VMEM_LIMIT = 56 * 1024 * 1024

CONV_HALO = 32
CONV_STRIP = 32


def _params(*sem):
    return pltpu.CompilerParams(dimension_semantics=sem, vmem_limit_bytes=VMEM_LIMIT)


def _rmsnorm_body(x_ref, g_ref, o_ref):
    x = x_ref[...]
    ms = jnp.mean(x * x, axis=-1, keepdims=True)
    o_ref[...] = (x * lax.rsqrt(ms + RMS_EPS) * g_ref[...]).astype(o_ref.dtype)


def rmsnorm(h, gains, layer, out_dtype, bm=512):
    t, d = h.shape
    return pl.pallas_call(
        _rmsnorm_body,
        grid=(t // bm,),
        in_specs=[pl.BlockSpec((bm, d), lambda i: (i, 0)),
                  pl.BlockSpec((None, 1, d), lambda i: (layer, 0, 0))],
        out_specs=pl.BlockSpec((bm, d), lambda i: (i, 0)),
        out_shape=jax.ShapeDtypeStruct((t, d), out_dtype),
        compiler_params=_params("parallel"),
        name="rmsnorm",
    )(h, gains)


def _scaled_rows(x, gain):
    rstd = lax.rsqrt(jnp.mean(x * x, axis=-1, keepdims=True) + RMS_EPS)
    return (x * gain).astype(BF16), rstd


def _proj_in_body(x_ref, g_ref, w_ref, o_ref, wbf_ref):
    @pl.when(pl.program_id(1) == 0)
    def _():
        wbf_ref[...] = w_ref[...].astype(BF16)

    xg, rstd = _scaled_rows(x_ref[...], g_ref[...])
    o_ref[...] = rstd * jnp.dot(xg, wbf_ref[...], preferred_element_type=F32)


def proj_in(h, gains, w_in, layer, bm=1024, bn=1024):
    t, d = h.shape
    n = w_in.shape[-1]
    return pl.pallas_call(
        _proj_in_body,
        grid=(n // bn, t // bm),
        in_specs=[pl.BlockSpec((bm, d), lambda j, i: (i, 0)),
                  pl.BlockSpec((None, 1, d), lambda j, i: (layer, 0, 0)),
                  pl.BlockSpec((None, d, bn), lambda j, i: (layer, 0, j))],
        out_specs=pl.BlockSpec((bm, bn), lambda j, i: (i, j)),
        out_shape=jax.ShapeDtypeStruct((t, n), F32),
        scratch_shapes=[pltpu.VMEM((d, bn), BF16)],
        compiler_params=_params("arbitrary", "arbitrary"),
        name="proj_in",
    )(h, gains, w_in)


def _conv_body(val_ref, gate_ref, w_ref, b_ref, lng_ref, lnb_ref, o_ref, u_ref, *, ts):
    s = pl.program_id(1)

    @pl.when(s == 0)
    def _():
        u_ref[0:CONV_HALO, :] = jnp.zeros((CONV_HALO, u_ref.shape[1]), F32)
        u_ref[CONV_HALO + ts:, :] = jnp.zeros((8, u_ref.shape[1]), F32)

    @pl.when(s > 0)
    def _():
        u_ref[0:CONV_HALO, :] = u_ref[ts:ts + CONV_HALO, :]

    u_ref[CONV_HALO:CONV_HALO + ts, :] = val_ref[...] * jax.nn.sigmoid(gate_ref[...])

    bias = b_ref[...]
    lng = lng_ref[...]
    lnb = lnb_ref[...]
    first_tap = CONV_HALO - (CONV_WIDTH - 1)

    n_win = CONV_STRIP + CONV_HALO + SUBLANES

    def strip(r, carry):
        r0 = pl.multiple_of(r * CONV_STRIP, CONV_STRIP)
        win = u_ref[pl.ds(r0, n_win), :]
        acc = jnp.zeros((CONV_STRIP, u_ref.shape[1]), F32)
        for sh in range(SUBLANES):
            shifted = win if sh == 0 else pltpu.roll(win, n_win - sh, axis=0)
            for a in range((CONV_HALO + SUBLANES) // SUBLANES):
                j = SUBLANES * a + sh - first_tap
                if 0 <= j < CONV_WIDTH:
                    acc = acc + w_ref[j:j + 1, :] * shifted[SUBLANES * a:SUBLANES * a + CONV_STRIP, :]
        c = acc + bias
        mu = jnp.mean(c, axis=-1, keepdims=True)
        cc = c - mu
        var = jnp.mean(cc * cc, axis=-1, keepdims=True)
        y = cc * lax.rsqrt(var + LN_EPS) * lng + lnb
        o_ref[pl.ds(r0, CONV_STRIP), :] = (y * jax.nn.sigmoid(y)).astype(o_ref.dtype)
        return carry

    lax.fori_loop(0, ts // CONV_STRIP, strip, 0)


def conv_group(z, conv_w, conv_b, ln_g, ln_b, layer, batch, seq, d_conv, ts=256):
    t = z.shape[0]
    nst = seq // ts
    vec = lambda: pl.BlockSpec((None, 1, d_conv), lambda b, s: (layer, 0, 0))
    return pl.pallas_call(
        functools.partial(_conv_body, ts=ts),
        grid=(batch, nst),
        in_specs=[pl.BlockSpec((ts, d_conv), lambda b, s: (b * nst + s, 0)),
                  pl.BlockSpec((ts, d_conv), lambda b, s: (b * nst + s, 1)),
                  pl.BlockSpec((None, CONV_WIDTH, d_conv), lambda b, s: (layer, 0, 0)),
                  vec(), vec(), vec()],
        out_specs=pl.BlockSpec((ts, d_conv), lambda b, s: (b * nst + s, 0)),
        out_shape=jax.ShapeDtypeStruct((t, d_conv), BF16),
        scratch_shapes=[pltpu.VMEM((CONV_HALO + ts + 8, d_conv), F32)],
        compiler_params=_params("arbitrary", "arbitrary"),
        name="conv_group",
    )(z, z, conv_w, conv_b, ln_g, ln_b)


_HALF_BLOCKS = (32, 16, 8, 4, 2, 1)


def _hgrn_body(lbl_ref, gn_ref, q_ref, f_ref, i_ref, g_ref, o_ref, b_scr, *, layer, n_chunks):
    lg = lbl_ref[...]
    ex = jnp.exp(lg - jnp.max(lg, axis=0, keepdims=True))
    sm = ex / jnp.sum(ex, axis=0, keepdims=True)
    lb = jnp.zeros((1, HG_DK), F32)
    for l in range(1, layer + 1):
        lb = lb + sm[l:l + 1, :]
    gn = gn_ref[...]

    c = CHUNK
    row = lax.broadcasted_iota(jnp.int32, (c, c), 0)
    col = lax.broadcasted_iota(jnp.int32, (c, c), 1)
    tril = (col <= row).astype(BF16)
    level = jnp.where(row == col, 0, -1)
    for lv, hb in enumerate(_HALF_BLOCKS):
        level = jnp.where((row // (2 * hb) == col // (2 * hb))
                          & (row % (2 * hb) >= hb) & (col % (2 * hb) < hb), lv + 1, level)
    sub = lax.broadcasted_iota(jnp.int32, (SUBLANES, HG_DK), 0)
    row_k = lax.broadcasted_iota(jnp.int32, (c, HG_DK), 0)
    nt = (((1,), (1,)), ((), ()))
    tn = (((0,), (0,)), ((), ()))

    def reference_rows(b, b_ref, hb):
        if hb == 1:
            return jnp.where(row_k % 2 == 0, b, pltpu.roll(b, 1, axis=0))
        bcast = lambda r: jnp.broadcast_to(b_ref[pl.ds(r, 1), :], (SUBLANES, HG_DK))
        groups = []
        for j in range(c // SUBLANES):
            r = SUBLANES * j
            if hb == 2:
                groups.append(jnp.where(sub < 4, bcast(r + 1), bcast(r + 5)))
            else:
                groups.append(bcast((r // (2 * hb)) * (2 * hb) + hb - 1))
        return jnp.concatenate(groups, axis=0)

    def chunk(ci, state_t, b_ref):
        r0 = pl.multiple_of(ci * c, c)
        q = q_ref[pl.ds(r0, c), :]
        f = lb + (1.0 - lb) * jax.nn.sigmoid(f_ref[pl.ds(r0, c), :])
        lf = jnp.log2(f)
        k = 1.0 - f
        v = i_ref[pl.ds(r0, c), :].astype(BF16)
        hi = lf.astype(BF16)
        rem = lf - hi.astype(F32)
        mid = rem.astype(BF16)
        lo = (rem - mid.astype(F32)).astype(BF16)
        b = (jnp.dot(tril, hi, preferred_element_type=F32)
             + jnp.dot(tril, mid, preferred_element_type=F32)
             + jnp.dot(tril, lo, preferred_element_type=F32))
        b_ref[...] = b
        scores = jnp.where(level == 0, lax.dot_general(q.astype(BF16), k.astype(BF16), nt,
                                                       preferred_element_type=F32), 0.0)
        for lv, hb in enumerate(_HALF_BLOCKS):
            e = jnp.exp2(-jnp.abs(b - reference_rows(b, b_ref, hb)))
            s_lv = lax.dot_general((q * e).astype(BF16), (k * e).astype(BF16), nt,
                                   preferred_element_type=F32)
            scores = jnp.where(level == lv + 1, s_lv, scores)
        o = (lax.dot_general((q * jnp.exp2(b)).astype(BF16), state_t.astype(BF16), nt,
                             preferred_element_type=F32)
             + jnp.dot(scores.astype(BF16), v, preferred_element_type=F32))
        b_last = b[c - 1:c, :]
        k_dec = (k * jnp.exp2(b_last - b)).astype(BF16)
        new_state_t = jnp.exp2(b_last) * state_t + lax.dot_general(
            v, k_dec, tn, preferred_element_type=F32)
        o = o * lax.rsqrt(jnp.mean(o * o, axis=-1, keepdims=True) + RMS_EPS) * gn
        g = g_ref[pl.ds(r0, c), :]
        o_ref[pl.ds(r0, c), :] = (o * (g * jax.nn.sigmoid(g))).astype(o_ref.dtype)
        return new_state_t

    def chunk_group(gi, state_t):
        for u in range(HGRN_UNROLL):
            state_t = chunk(gi * HGRN_UNROLL + u, state_t, b_scr.at[u])
        return state_t

    lax.fori_loop(0, n_chunks // HGRN_UNROLL, chunk_group, jnp.zeros((HG_DV, HG_DK), F32))


def hgrn_group(z, lb_logits, hg_norm, layer, batch, seq, heads, col0):
    t = z.shape[0]
    depth = lb_logits.shape[0]
    c0 = col0 // LANES
    zspec = lambda off: pl.BlockSpec((seq, LANES), lambda b, h: (b, c0 + off * heads + h))
    return pl.pallas_call(
        functools.partial(_hgrn_body, layer=layer, n_chunks=seq // CHUNK),
        grid=(batch, heads),
        in_specs=[pl.BlockSpec((depth, HG_DK), lambda b, h: (0, h)),
                  pl.BlockSpec((None, 1, HG_DV), lambda b, h: (layer, 0, 0)),
                  zspec(0), zspec(1), zspec(2), zspec(3)],
        out_specs=pl.BlockSpec((seq, HG_DV), lambda b, h: (b, h)),
        out_shape=jax.ShapeDtypeStruct((t, heads * HG_DV), BF16),
        scratch_shapes=[pltpu.VMEM((HGRN_UNROLL, CHUNK, HG_DK), F32)],
        compiler_params=_params("parallel", "parallel"),
        name="hgrn_group",
    )(lb_logits, hg_norm, z, z, z, z)


def _proj_out_body(yc_ref, yh_ref, w_ref, h_ref, o_ref, wbf_ref, *, d_conv):
    @pl.when(pl.program_id(1) == 0)
    def _():
        wbf_ref[...] = w_ref[...].astype(BF16)

    y = (jnp.dot(yc_ref[...], wbf_ref[0:d_conv, :], preferred_element_type=F32)
         + jnp.dot(yh_ref[...], wbf_ref[d_conv:, :], preferred_element_type=F32))
    o_ref[...] = h_ref[...] + y


def proj_out(yc, yh, w_out, h, layer, bm=1024, bn=512):
    t, d_conv = yc.shape
    d_hg = yh.shape[1]
    d = h.shape[1]
    return pl.pallas_call(
        functools.partial(_proj_out_body, d_conv=d_conv),
        grid=(d // bn, t // bm),
        in_specs=[pl.BlockSpec((bm, d_conv), lambda j, i: (i, 0)),
                  pl.BlockSpec((bm, d_hg), lambda j, i: (i, 0)),
                  pl.BlockSpec((None, d_conv + d_hg, bn), lambda j, i: (layer, 0, j)),
                  pl.BlockSpec((bm, bn), lambda j, i: (i, j))],
        out_specs=pl.BlockSpec((bm, bn), lambda j, i: (i, j)),
        out_shape=jax.ShapeDtypeStruct((t, d), F32),
        scratch_shapes=[pltpu.VMEM((d_conv + d_hg, bn), BF16)],
        compiler_params=_params("arbitrary", "arbitrary"),
        name="proj_out",
    )(yc, yh, w_out, h)


def _ffn_gu_body(te_ref, tf_ref, nu_ref, x_ref, *rest, norm):
    if norm:
        g_ref, wg_ref, wu_ref, o_ref, wgbf_ref, wubf_ref = rest
    else:
        wg_ref, wu_ref, o_ref, wgbf_ref, wubf_ref = rest
    i = pl.program_id(1)

    @pl.when(i < nu_ref[0])
    def _():
        @pl.when(tf_ref[i] == 1)
        def _():
            wgbf_ref[...] = wg_ref[...].astype(BF16)
            wubf_ref[...] = wu_ref[...].astype(BF16)

        if norm:
            x, rstd = _scaled_rows(x_ref[...], g_ref[...])
        else:
            x = x_ref[...].astype(BF16)
        a = jnp.dot(x, wgbf_ref[...], preferred_element_type=F32)
        b = jnp.dot(x, wubf_ref[...], preferred_element_type=F32)
        if norm:
            a = rstd * a
            b = rstd * b
        o_ref[...] = (a * jax.nn.sigmoid(a) * b).astype(o_ref.dtype)

    @pl.when(i >= nu_ref[0])
    def _():
        o_ref[...] = jnp.zeros(o_ref.shape, o_ref.dtype)


def ffn_gate_up(x, w_gu, tile_expert, tile_first, n_used, bm, bn=512, gains=None, layer=0):
    r, d = x.shape
    f = w_gu.shape[-1] // 2
    nfb = f // bn
    norm = gains is not None
    row = lambda j, i, te, tf, nu: (jnp.minimum(i, nu[0] - 1), 0)
    in_specs = [pl.BlockSpec((bm, d), row)]
    args = [x]
    if norm:
        in_specs.append(pl.BlockSpec((None, 1, d), lambda j, i, te, tf, nu: (layer, 0, 0)))
        args.append(gains)
    in_specs += [pl.BlockSpec((None, d, bn), lambda j, i, te, tf, nu: (te[i], 0, j)),
                 pl.BlockSpec((None, d, bn), lambda j, i, te, tf, nu: (te[i], 0, nfb + j))]
    grid_spec = pltpu.PrefetchScalarGridSpec(
        num_scalar_prefetch=3,
        grid=(nfb, r // bm),
        in_specs=in_specs,
        out_specs=pl.BlockSpec((bm, bn), lambda j, i, te, tf, nu: (i, j)),
        scratch_shapes=[pltpu.VMEM((d, bn), BF16), pltpu.VMEM((d, bn), BF16)],
    )
    return pl.pallas_call(
        functools.partial(_ffn_gu_body, norm=norm),
        grid_spec=grid_spec,
        out_shape=jax.ShapeDtypeStruct((r, f), BF16),
        compiler_params=_params("arbitrary", "arbitrary"),
        name="ffn_gate_up",
    )(tile_expert, tile_first, n_used, *args, w_gu, w_gu)


def _ffn_dn_body(te_ref, tf_ref, nu_ref, x_ref, w_ref, *rest, residual):
    if residual:
        h_ref, o_ref, wbf_ref = rest
    else:
        o_ref, wbf_ref = rest
    i = pl.program_id(1)

    @pl.when(i < nu_ref[0])
    def _():
        @pl.when(tf_ref[i] == 1)
        def _():
            wbf_ref[...] = w_ref[...].astype(BF16)

        y = jnp.dot(x_ref[...], wbf_ref[...], preferred_element_type=F32)
        if residual:
            y = h_ref[...] + y
        o_ref[...] = y

    @pl.when(i >= nu_ref[0])
    def _():
        o_ref[...] = jnp.zeros(o_ref.shape, o_ref.dtype)


def ffn_down(x, w_dn, tile_expert, tile_first, n_used, bm, h=None, bn=512):
    r, f = x.shape
    d = w_dn.shape[-1]
    row = lambda j, i, te, tf, nu: (jnp.minimum(i, nu[0] - 1), 0)
    tile = lambda j, i, te, tf, nu: (i, j)
    in_specs = [pl.BlockSpec((bm, f), row),
                pl.BlockSpec((None, f, bn), lambda j, i, te, tf, nu: (te[i], 0, j))]
    args = [x, w_dn]
    if h is not None:
        in_specs.append(pl.BlockSpec((bm, bn), tile))
        args.append(h)
    grid_spec = pltpu.PrefetchScalarGridSpec(
        num_scalar_prefetch=3,
        grid=(d // bn, r // bm),
        in_specs=in_specs,
        out_specs=pl.BlockSpec((bm, bn), tile),
        scratch_shapes=[pltpu.VMEM((f, bn), BF16)],
    )
    return pl.pallas_call(
        functools.partial(_ffn_dn_body, residual=h is not None),
        grid_spec=grid_spec,
        out_shape=jax.ShapeDtypeStruct((r, d), F32),
        compiler_params=_params("arbitrary", "arbitrary"),
        name="ffn_down",
    )(tile_expert, tile_first, n_used, *args)


def _router_body(h_ref, g_ref, w_ref, b_ref, xn_ref, meta_ref, cnt_ref, carry_ref, *, tm):
    @pl.when(pl.program_id(0) == 0)
    def _():
        carry_ref[...] = jnp.zeros(carry_ref.shape, F32)

    h = h_ref[...]
    xn = h * lax.rsqrt(jnp.mean(h * h, axis=-1, keepdims=True) + RMS_EPS) * g_ref[...]
    xn_ref[...] = xn
    lane = lax.broadcasted_iota(jnp.int32, (tm, LANES), 1)
    logits = jnp.dot(xn, w_ref[...], precision=lax.Precision.HIGHEST,
                     preferred_element_type=F32) + b_ref[...]
    neg = jnp.float32(-jnp.inf)
    logits = jnp.where(lane < N_EXPERTS, logits, neg)
    m1 = jnp.max(logits, axis=-1, keepdims=True)
    i1 = jnp.min(jnp.where(logits == m1, lane, LANES), axis=-1, keepdims=True)
    rest = jnp.where(lane == i1, neg, logits)
    m2 = jnp.max(rest, axis=-1, keepdims=True)
    i2 = jnp.min(jnp.where(rest == m2, lane, LANES), axis=-1, keepdims=True)
    e2 = jnp.exp(m2 - m1)
    g1 = 1.0 / (1.0 + e2)
    g2 = e2 / (1.0 + e2)
    hot1 = lane == i1
    hot2 = lane == i2
    hot = (hot1 | hot2).astype(BF16)
    row = lax.broadcasted_iota(jnp.int32, (tm, tm), 0)
    col = lax.broadcasted_iota(jnp.int32, (tm, tm), 1)
    before = (col < row).astype(BF16)
    carry = carry_ref[0:1, :]
    rank = jnp.dot(before, hot, preferred_element_type=F32) + carry
    r1 = jnp.sum(jnp.where(hot1, rank, 0.0), axis=-1, keepdims=True)
    r2 = jnp.sum(jnp.where(hot2, rank, 0.0), axis=-1, keepdims=True)
    total = carry + jnp.sum(hot.astype(F32), axis=0, keepdims=True)
    carry_ref[...] = jnp.broadcast_to(total, carry_ref.shape)
    cnt_ref[...] = jnp.broadcast_to(total, cnt_ref.shape)
    meta = jnp.where(lane == 0, i1.astype(F32), 0.0)
    meta = jnp.where(lane == 1, i2.astype(F32), meta)
    meta = jnp.where(lane == 2, g1, meta)
    meta = jnp.where(lane == 3, g2, meta)
    meta = jnp.where(lane == 4, r1, meta)
    meta = jnp.where(lane == 5, r2, meta)
    meta_ref[...] = meta


def router(h, gains, layer, w_pad, b_pad, tm=512):
    t, d = h.shape
    return pl.pallas_call(
        functools.partial(_router_body, tm=tm),
        grid=(t // tm,),
        in_specs=[pl.BlockSpec((tm, d), lambda i: (i, 0)),
                  pl.BlockSpec((None, 1, d), lambda i: (layer, 0, 0)),
                  pl.BlockSpec((d, LANES), lambda i: (0, 0)),
                  pl.BlockSpec((1, LANES), lambda i: (0, 0))],
        out_specs=[pl.BlockSpec((tm, d), lambda i: (i, 0)),
                   pl.BlockSpec((tm, LANES), lambda i: (i, 0)),
                   pl.BlockSpec((SUBLANES, LANES), lambda i: (0, 0))],
        out_shape=[jax.ShapeDtypeStruct((t, d), F32),
                   jax.ShapeDtypeStruct((t, LANES), F32),
                   jax.ShapeDtypeStruct((SUBLANES, LANES), F32)],
        scratch_shapes=[pltpu.VMEM((SUBLANES, LANES), F32)],
        compiler_params=_params("arbitrary"),
        name="router",
    )(h, gains, w_pad, b_pad)


def _dispatch_body(pos_ref, x_ref, init_ref, o_ref, sem, *, tm):
    del init_ref
    base = pl.program_id(0) * (TOP_K * tm)

    def row_copy(r, k):
        p = pos_ref[base + TOP_K * r + k]
        return pltpu.make_async_copy(x_ref.at[pl.ds(r, 1)], o_ref.at[pl.ds(p, 1)], sem)

    def start(r, carry):
        for k in range(TOP_K):
            row_copy(r, k).start()
        return carry

    def wait(r, carry):
        for k in range(TOP_K):
            row_copy(r, k).wait()
        return carry

    lax.fori_loop(0, tm, start, 0)
    lax.fori_loop(0, tm, wait, 0)


def dispatch(xn, pos, rows, tm=512):
    t, d = xn.shape
    grid_spec = pltpu.PrefetchScalarGridSpec(
        num_scalar_prefetch=1,
        grid=(t // tm,),
        in_specs=[pl.BlockSpec((tm, d), lambda i, pos: (i, 0)),
                  pl.BlockSpec(memory_space=pl.ANY)],
        out_specs=pl.BlockSpec(memory_space=pl.ANY),
        scratch_shapes=[pltpu.SemaphoreType.DMA(())],
    )
    return pl.pallas_call(
        functools.partial(_dispatch_body, tm=tm),
        grid_spec=grid_spec,
        out_shape=jax.ShapeDtypeStruct((rows, d), xn.dtype),
        input_output_aliases={2: 0},
        compiler_params=_params("arbitrary"),
        name="moe_dispatch",
    )(pos, xn, jnp.zeros((rows, d), xn.dtype))


def _combine_body(pos_ref, y_ref, meta_ref, h_ref, o_ref, buf_ref, sem, *, tm):
    base = pl.program_id(0) * (TOP_K * tm)

    def row_copy(r, k):
        p = pos_ref[base + TOP_K * r + k]
        return pltpu.make_async_copy(y_ref.at[pl.ds(p, 1)], buf_ref.at[k, pl.ds(r, 1)], sem)

    def start(r, carry):
        for k in range(TOP_K):
            row_copy(r, k).start()
        return carry

    def wait(r, carry):
        for k in range(TOP_K):
            row_copy(r, k).wait()
        return carry

    lax.fori_loop(0, tm, start, 0)
    lax.fori_loop(0, tm, wait, 0)
    meta = meta_ref[...]
    o_ref[...] = h_ref[...] + meta[:, 2:3] * buf_ref[0] + meta[:, 3:4] * buf_ref[1]


def combine(y, pos, meta, h, tm=256):
    t, d = h.shape
    grid_spec = pltpu.PrefetchScalarGridSpec(
        num_scalar_prefetch=1,
        grid=(t // tm,),
        in_specs=[pl.BlockSpec(memory_space=pl.ANY),
                  pl.BlockSpec((tm, LANES), lambda i, pos: (i, 0)),
                  pl.BlockSpec((tm, d), lambda i, pos: (i, 0))],
        out_specs=pl.BlockSpec((tm, d), lambda i, pos: (i, 0)),
        scratch_shapes=[pltpu.VMEM((TOP_K, tm, d), F32), pltpu.SemaphoreType.DMA(())],
    )
    return pl.pallas_call(
        functools.partial(_combine_body, tm=tm),
        grid_spec=grid_spec,
        out_shape=jax.ShapeDtypeStruct((t, d), F32),
        compiler_params=_params("arbitrary"),
        name="moe_combine",
    )(pos, y, meta, h)


def moe_ffn(h, gains, layer, router_w, router_b, w_gu, w_dn, moe_layer, bm=512):
    t, d = h.shape
    e = N_EXPERTS
    w_pad = jnp.pad(router_w[moe_layer], ((0, 0), (0, LANES - e)))
    b_pad = jnp.pad(router_b[moe_layer], (0, LANES - e)).reshape(1, LANES)
    xn, meta, counts = router(h, gains, layer, w_pad, b_pad)
    counts = counts[0, :e].astype(jnp.int32)
    tiles = (counts + bm - 1) // bm
    tile_end = jnp.cumsum(tiles)
    group_start = (tile_end - tiles) * bm
    n_tiles = (TOP_K * t + e * (bm - 1)) // bm
    n_used = tile_end[-1:]
    tile_ids = jnp.arange(n_tiles, dtype=jnp.int32)
    tile_expert = jnp.sum(tile_ids[:, None] >= tile_end[None, :], axis=1).astype(jnp.int32)
    tile_expert = jnp.minimum(tile_expert, tile_expert[n_used[0] - 1])
    tile_first = jnp.concatenate([jnp.ones((1,), jnp.int32),
                                  (tile_expert[1:] != tile_expert[:-1]).astype(jnp.int32)])
    sel = meta[:, 0:TOP_K].astype(jnp.int32)
    rank = meta[:, 4:4 + TOP_K].astype(jnp.int32)
    pos = (group_start[sel] + rank).reshape(-1)
    x_sorted = dispatch(xn, pos, n_tiles * bm)
    te = tile_expert + moe_layer * e
    mid = ffn_gate_up(x_sorted, w_gu, te, tile_first, n_used, bm)
    y = ffn_down(mid, w_dn, te, tile_first, n_used, bm)
    return combine(y, pos, meta, h)


def dense_ffn(h, gains, layer, w_gu, w_dn, dense_layer, bm_up=1024, bm_down=512):
    t = h.shape[0]

    def one_group(bm):
        n_tiles = t // bm
        return (jnp.full((n_tiles,), dense_layer, jnp.int32),
                jnp.zeros((n_tiles,), jnp.int32).at[0].set(1),
                jnp.full((1,), n_tiles, jnp.int32))

    mid = ffn_gate_up(h, w_gu, *one_group(bm_up), bm_up, gains=gains, layer=layer)
    return ffn_down(mid, w_dn, *one_group(bm_down), bm_down, h=h)


def _ple_body(p_ref, x_ref, g_ref, wp_ref, wg_ref, h_ref, o_ref, wpbf_ref, wgbf_ref):
    @pl.when(pl.program_id(1) == 0)
    def _():
        wpbf_ref[...] = wp_ref[...].astype(BF16)
        wgbf_ref[...] = wg_ref[...].astype(BF16)

    e = jnp.dot(p_ref[...].astype(BF16), wpbf_ref[...], preferred_element_type=F32)
    xg, rstd = _scaled_rows(x_ref[...], g_ref[...])
    gate = jax.nn.sigmoid(rstd * jnp.dot(xg, wgbf_ref[...], preferred_element_type=F32))
    o_ref[...] = h_ref[...] + e * gate


def ple(h, gains, p, ple_w, gate_w, layer, bm=1024, bn=512):
    t, d = h.shape
    pd = p.shape[-1]
    return pl.pallas_call(
        _ple_body,
        grid=(d // bn, t // bm),
        in_specs=[pl.BlockSpec((None, bm, pd), lambda j, i: (layer, i, 0)),
                  pl.BlockSpec((bm, d), lambda j, i: (i, 0)),
                  pl.BlockSpec((None, 1, d), lambda j, i: (layer, 0, 0)),
                  pl.BlockSpec((None, pd, bn), lambda j, i: (layer, 0, j)),
                  pl.BlockSpec((None, d, bn), lambda j, i: (layer, 0, j)),
                  pl.BlockSpec((bm, bn), lambda j, i: (i, j))],
        out_specs=pl.BlockSpec((bm, bn), lambda j, i: (i, j)),
        out_shape=jax.ShapeDtypeStruct((t, d), F32),
        scratch_shapes=[pltpu.VMEM((pd, bn), BF16), pltpu.VMEM((d, bn), BF16)],
        compiler_params=_params("arbitrary", "arbitrary"),
        name="ple",
    )(p, h, gains, ple_w, gate_w, h)


def kernel(x, p, mix_norm, w_in, conv_w, conv_b, conv_ln_g, conv_ln_b, hg_lb_logits, hg_norm, w_out, ffn_norm, ffn_w_gu, ffn_w_dn, router_w, router_b, moe_w_gu, moe_w_dn, ple_w, ple_norm, ple_gate_w, final_norm):
    batch, seq, d = x.shape
    depth = w_in.shape[0]
    t = batch * seq
    d_conv = conv_w.shape[-1]
    heads = (w_out.shape[1] - d_conv) // HG_DV
    col_hg = 2 * d_conv

    h = x.reshape(t, d)
    p2 = p.reshape(depth, t, p.shape[-1])
    vec3 = lambda a: a.reshape(a.shape[0], 1, a.shape[-1])
    mix_g, ffn_g, ple_g = vec3(mix_norm), vec3(ffn_norm), vec3(ple_norm)
    cb, lng, lnb, gn = vec3(conv_b), vec3(conv_ln_g), vec3(conv_ln_b), vec3(hg_norm)
    moe_gu = moe_w_gu.reshape((-1,) + moe_w_gu.shape[2:])
    moe_dn = moe_w_dn.reshape((-1,) + moe_w_dn.shape[2:])

    for l in range(depth):
        z = proj_in(h, mix_g, w_in, l)
        yc = conv_group(z, conv_w, cb, lng, lnb, l, batch, seq, d_conv)
        yh = hgrn_group(z, hg_lb_logits, gn, l, batch, seq, heads, col_hg)
        h = proj_out(yc, yh, w_out, h, l)
        if l % 2 == 0:
            h = dense_ffn(h, ffn_g, l, ffn_w_gu, ffn_w_dn, l // 2)
        else:
            h = moe_ffn(h, ffn_g, l, router_w, router_b, moe_gu, moe_dn, l // 2)
        h = ple(h, ple_g, p2, ple_w, ple_gate_w, l)
    out = rmsnorm(h, final_norm.reshape(1, 1, d), 0, F32)
    return out.reshape(batch, seq, d)
```

```python
import functools

import jax
import jax.numpy as jnp
from jax import lax
from jax.experimental import pallas as pl
from jax.experimental.pallas import tpu as pltpu

F32 = jnp.float32
BF16 = jnp.bfloat16

RMS_EPS = 1e-6
LN_EPS = 1e-5
CONV_WIDTH = 31
HG_DK = 128
HG_DV = 128
CHUNK = 64
N_EXPERTS = 8
TOP_K = 2

LANES = 128
SUBLANES = 8
HGRN_UNROLL = 8
VMEM_LIMIT = 56 * 1024 * 1024

CONV_HALO = 32
CONV_STRIP = 32


def _params(*sem):
    return pltpu.CompilerParams(dimension_semantics=sem, vmem_limit_bytes=VMEM_LIMIT)


def _rmsnorm_body(x_ref, g_ref, o_ref):
    x = x_ref[...]
    ms = jnp.mean(x * x, axis=-1, keepdims=True)
    o_ref[...] = (x * lax.rsqrt(ms + RMS_EPS) * g_ref[...]).astype(o_ref.dtype)


def rmsnorm(h, gains, layer, out_dtype, bm=512):
    t, d = h.shape
    return pl.pallas_call(
        _rmsnorm_body,
        grid=(t // bm,),
        in_specs=[pl.BlockSpec((bm, d), lambda i: (i, 0)),
                  pl.BlockSpec((None, 1, d), lambda i: (layer, 0, 0))],
        out_specs=pl.BlockSpec((bm, d), lambda i: (i, 0)),
        out_shape=jax.ShapeDtypeStruct((t, d), out_dtype),
        compiler_params=_params("parallel"),
        name="rmsnorm",
    )(h, gains)


def _scaled_rows(x, gain):
    rstd = lax.rsqrt(jnp.mean(x * x, axis=-1, keepdims=True) + RMS_EPS)
    return (x * gain).astype(BF16), rstd


def _proj_in_body(x_ref, g_ref, w_ref, o_ref, wbf_ref):
    @pl.when(pl.program_id(1) == 0)
    def _():
        wbf_ref[...] = w_ref[...].astype(BF16)

    xg, rstd = _scaled_rows(x_ref[...], g_ref[...])
    o_ref[...] = rstd * jnp.dot(xg, wbf_ref[...], preferred_element_type=F32)


def proj_in(h, gains, w_in, layer, bm=1024, bn=1024):
    t, d = h.shape
    n = w_in.shape[-1]
    return pl.pallas_call(
        _proj_in_body,
        grid=(n // bn, t // bm),
        in_specs=[pl.BlockSpec((bm, d), lambda j, i: (i, 0)),
                  pl.BlockSpec((None, 1, d), lambda j, i: (layer, 0, 0)),
                  pl.BlockSpec((None, d, bn), lambda j, i: (layer, 0, j))],
        out_specs=pl.BlockSpec((bm, bn), lambda j, i: (i, j)),
        out_shape=jax.ShapeDtypeStruct((t, n), F32),
        scratch_shapes=[pltpu.VMEM((d, bn), BF16)],
        compiler_params=_params("arbitrary", "arbitrary"),
        name="proj_in",
    )(h, gains, w_in)


def _conv_body(val_ref, gate_ref, w_ref, b_ref, lng_ref, lnb_ref, o_ref, u_ref, *, ts):
    s = pl.program_id(1)

    @pl.when(s == 0)
    def _():
        u_ref[0:CONV_HALO, :] = jnp.zeros((CONV_HALO, u_ref.shape[1]), F32)
        u_ref[CONV_HALO + ts:, :] = jnp.zeros((8, u_ref.shape[1]), F32)

    @pl.when(s > 0)
    def _():
        u_ref[0:CONV_HALO, :] = u_ref[ts:ts + CONV_HALO, :]

    u_ref[CONV_HALO:CONV_HALO + ts, :] = val_ref[...] * jax.nn.sigmoid(gate_ref[...])

    bias = b_ref[...]
    lng = lng_ref[...]
    lnb = lnb_ref[...]
    first_tap = CONV_HALO - (CONV_WIDTH - 1)

    n_win = CONV_STRIP + CONV_HALO + SUBLANES

    def strip(r, carry):
        r0 = pl.multiple_of(r * CONV_STRIP, CONV_STRIP)
        win = u_ref[pl.ds(r0, n_win), :]
        acc = jnp.zeros((CONV_STRIP, u_ref.shape[1]), F32)
        for sh in range(SUBLANES):
            shifted = win if sh == 0 else pltpu.roll(win, n_win - sh, axis=0)
            for a in range((CONV_HALO + SUBLANES) // SUBLANES):
                j = SUBLANES * a + sh - first_tap
                if 0 <= j < CONV_WIDTH:
                    acc = acc + w_ref[j:j + 1, :] * shifted[SUBLANES * a:SUBLANES * a + CONV_STRIP, :]
        c = acc + bias
        mu = jnp.mean(c, axis=-1, keepdims=True)
        cc = c - mu
        var = jnp.mean(cc * cc, axis=-1, keepdims=True)
        y = cc * lax.rsqrt(var + LN_EPS) * lng + lnb
        o_ref[pl.ds(r0, CONV_STRIP), :] = (y * jax.nn.sigmoid(y)).astype(o_ref.dtype)
        return carry

    lax.fori_loop(0, ts // CONV_STRIP, strip, 0)


def conv_group(z, conv_w, conv_b, ln_g, ln_b, layer, batch, seq, d_conv, ts=256):
    t = z.shape[0]
    nst = seq // ts
    vec = lambda: pl.BlockSpec((None, 1, d_conv), lambda b, s: (layer, 0, 0))
    return pl.pallas_call(
        functools.partial(_conv_body, ts=ts),
        grid=(batch, nst),
        in_specs=[pl.BlockSpec((ts, d_conv), lambda b, s: (b * nst + s, 0)),
                  pl.BlockSpec((ts, d_conv), lambda b, s: (b * nst + s, 1)),
                  pl.BlockSpec((None, CONV_WIDTH, d_conv), lambda b, s: (layer, 0, 0)),
                  vec(), vec(), vec()],
        out_specs=pl.BlockSpec((ts, d_conv), lambda b, s: (b * nst + s, 0)),
        out_shape=jax.ShapeDtypeStruct((t, d_conv), BF16),
        scratch_shapes=[pltpu.VMEM((CONV_HALO + ts + 8, d_conv), F32)],
        compiler_params=_params("arbitrary", "arbitrary"),
        name="conv_group",
    )(z, z, conv_w, conv_b, ln_g, ln_b)


_HALF_BLOCKS = (32, 16, 8, 4, 2, 1)


def _hgrn_body(lbl_ref, gn_ref, q_ref, f_ref, i_ref, g_ref, o_ref, b_scr, *, layer, n_chunks):
    lg = lbl_ref[...]
    ex = jnp.exp(lg - jnp.max(lg, axis=0, keepdims=True))
    sm = ex / jnp.sum(ex, axis=0, keepdims=True)
    lb = jnp.zeros((1, HG_DK), F32)
    for l in range(1, layer + 1):
        lb = lb + sm[l:l + 1, :]
    gn = gn_ref[...]

    c = CHUNK
    row = lax.broadcasted_iota(jnp.int32, (c, c), 0)
    col = lax.broadcasted_iota(jnp.int32, (c, c), 1)
    tril = (col <= row).astype(BF16)
    level = jnp.where(row == col, 0, -1)
    for lv, hb in enumerate(_HALF_BLOCKS):
        level = jnp.where((row // (2 * hb) == col // (2 * hb))
                          & (row % (2 * hb) >= hb) & (col % (2 * hb) < hb), lv + 1, level)
    sub = lax.broadcasted_iota(jnp.int32, (SUBLANES, HG_DK), 0)
    row_k = lax.broadcasted_iota(jnp.int32, (c, HG_DK), 0)
    nt = (((1,), (1,)), ((), ()))
    tn = (((0,), (0,)), ((), ()))

    def reference_rows(b, b_ref, hb):
        if hb == 1:
            return jnp.where(row_k % 2 == 0, b, pltpu.roll(b, 1, axis=0))
        bcast = lambda r: jnp.broadcast_to(b_ref[pl.ds(r, 1), :], (SUBLANES, HG_DK))
        groups = []
        for j in range(c // SUBLANES):
            r = SUBLANES * j
            if hb == 2:
                groups.append(jnp.where(sub < 4, bcast(r + 1), bcast(r + 5)))
            else:
                groups.append(bcast((r // (2 * hb)) * (2 * hb) + hb - 1))
        return jnp.concatenate(groups, axis=0)

    def chunk(ci, state_t, b_ref):
        r0 = pl.multiple_of(ci * c, c)
        q = q_ref[pl.ds(r0, c), :]
        f = lb + (1.0 - lb) * jax.nn.sigmoid(f_ref[pl.ds(r0, c), :])
        lf = jnp.log2(f)
        k = 1.0 - f
        v = i_ref[pl.ds(r0, c), :].astype(BF16)
        hi = lf.astype(BF16)
        rem = lf - hi.astype(F32)
        mid = rem.astype(BF16)
        lo = (rem - mid.astype(F32)).astype(BF16)
        b = (jnp.dot(tril, hi, preferred_element_type=F32)
             + jnp.dot(tril, mid, preferred_element_type=F32)
             + jnp.dot(tril, lo, preferred_element_type=F32))
        b_ref[...] = b
        scores = jnp.where(level == 0, lax.dot_general(q.astype(BF16), k.astype(BF16), nt,
                                                       preferred_element_type=F32), 0.0)
        for lv, hb in enumerate(_HALF_BLOCKS):
            e = jnp.exp2(-jnp.abs(b - reference_rows(b, b_ref, hb)))
            s_lv = lax.dot_general((q * e).astype(BF16), (k * e).astype(BF16), nt,
                                   preferred_element_type=F32)
            scores = jnp.where(level == lv + 1, s_lv, scores)
        o = (lax.dot_general((q * jnp.exp2(b)).astype(BF16), state_t.astype(BF16), nt,
                             preferred_element_type=F32)
             + jnp.dot(scores.astype(BF16), v, preferred_element_type=F32))
        b_last = b[c - 1:c, :]
        k_dec = (k * jnp.exp2(b_last - b)).astype(BF16)
        new_state_t = jnp.exp2(b_last) * state_t + lax.dot_general(
            v, k_dec, tn, preferred_element_type=F32)
        o = o * lax.rsqrt(jnp.mean(o * o, axis=-1, keepdims=True) + RMS_EPS) * gn
        g = g_ref[pl.ds(r0, c), :]
        o_ref[pl.ds(r0, c), :] = (o * (g * jax.nn.sigmoid(g))).astype(o_ref.dtype)
        return new_state_t

    def chunk_group(gi, state_t):
        for u in range(HGRN_UNROLL):
            state_t = chunk(gi * HGRN_UNROLL + u, state_t, b_scr.at[u])
        return state_t

    lax.fori_loop(0, n_chunks // HGRN_UNROLL, chunk_group, jnp.zeros((HG_DV, HG_DK), F32))


def hgrn_group(z, lb_logits, hg_norm, layer, batch, seq, heads, col0):
    t = z.shape[0]
    depth = lb_logits.shape[0]
    c0 = col0 // LANES
    zspec = lambda off: pl.BlockSpec((seq, LANES), lambda b, h: (b, c0 + off * heads + h))
    return pl.pallas_call(
        functools.partial(_hgrn_body, layer=layer, n_chunks=seq // CHUNK),
        grid=(batch, heads),
        in_specs=[pl.BlockSpec((depth, HG_DK), lambda b, h: (0, h)),
                  pl.BlockSpec((None, 1, HG_DV), lambda b, h: (layer, 0, 0)),
                  zspec(0), zspec(1), zspec(2), zspec(3)],
        out_specs=pl.BlockSpec((seq, HG_DV), lambda b, h: (b, h)),
        out_shape=jax.ShapeDtypeStruct((t, heads * HG_DV), BF16),
        scratch_shapes=[pltpu.VMEM((HGRN_UNROLL, CHUNK, HG_DK), F32)],
        compiler_params=_params("parallel", "parallel"),
        name="hgrn_group",
    )(lb_logits, hg_norm, z, z, z, z)


def _proj_out_body(yc_ref, yh_ref, w_ref, h_ref, o_ref, wbf_ref, *, d_conv):
    @pl.when(pl.program_id(1) == 0)
    def _():
        wbf_ref[...] = w_ref[...].astype(BF16)

    y = (jnp.dot(yc_ref[...], wbf_ref[0:d_conv, :], preferred_element_type=F32)
         + jnp.dot(yh_ref[...], wbf_ref[d_conv:, :], preferred_element_type=F32))
    o_ref[...] = h_ref[...] + y


def proj_out(yc, yh, w_out, h, layer, bm=1024, bn=1024):
    t, d_conv = yc.shape
    d_hg = yh.shape[1]
    d = h.shape[1]
    return pl.pallas_call(
        functools.partial(_proj_out_body, d_conv=d_conv),
        grid=(d // bn, t // bm),
        in_specs=[pl.BlockSpec((bm, d_conv), lambda j, i: (i, 0)),
                  pl.BlockSpec((bm, d_hg), lambda j, i: (i, 0)),
                  pl.BlockSpec((None, d_conv + d_hg, bn), lambda j, i: (layer, 0, j)),
                  pl.BlockSpec((bm, bn), lambda j, i: (i, j))],
        out_specs=pl.BlockSpec((bm, bn), lambda j, i: (i, j)),
        out_shape=jax.ShapeDtypeStruct((t, d), F32),
        scratch_shapes=[pltpu.VMEM((d_conv + d_hg, bn), BF16)],
        compiler_params=_params("arbitrary", "arbitrary"),
        name="proj_out",
    )(yc, yh, w_out, h)


def _for_tile_rows(rows, bm, compute, o_ref):
    half = bm // 2

    @pl.when(rows > half)
    def _():
        compute(bm)

    @pl.when((rows > 0) & (rows <= half))
    def _():
        compute(half)
        o_ref[half:, :] = jnp.zeros((bm - half, o_ref.shape[1]), o_ref.dtype)

    @pl.when(rows == 0)
    def _():
        o_ref[...] = jnp.zeros(o_ref.shape, o_ref.dtype)


def _ffn_gu_body(te_ref, tf_ref, tr_ref, nu_ref, x_ref, *rest, norm):
    if norm:
        g_ref, wg_ref, wu_ref, o_ref, wgbf_ref, wubf_ref = rest
    else:
        wg_ref, wu_ref, o_ref, wgbf_ref, wubf_ref = rest
    i = pl.program_id(1)
    rows = tr_ref[i]

    @pl.when((rows > 0) & (tf_ref[i] == 1))
    def _():
        wgbf_ref[...] = wg_ref[...].astype(BF16)
        wubf_ref[...] = wu_ref[...].astype(BF16)

    def compute(n):
        if norm:
            x, rstd = _scaled_rows(x_ref[0:n, :], g_ref[...])
        else:
            x = x_ref[0:n, :]
        a = jnp.dot(x, wgbf_ref[...], preferred_element_type=F32)
        b = jnp.dot(x, wubf_ref[...], preferred_element_type=F32)
        if norm:
            a = rstd * a
            b = rstd * b
        o_ref[0:n, :] = (a * jax.nn.sigmoid(a) * b).astype(o_ref.dtype)

    _for_tile_rows(rows, x_ref.shape[0], compute, o_ref)


def ffn_gate_up(x, w_gu, plan, bm, bn=512, gains=None, layer=0):
    r, d = x.shape
    f = w_gu.shape[-1] // 2
    nfb = f // bn
    norm = gains is not None
    row = lambda j, i, te, tf, tr, nu: (jnp.minimum(i, nu[0] - 1), 0)
    in_specs = [pl.BlockSpec((bm, d), row)]
    args = [x]
    if norm:
        in_specs.append(pl.BlockSpec((None, 1, d), lambda j, i, te, tf, tr, nu: (layer, 0, 0)))
        args.append(gains)
    in_specs += [pl.BlockSpec((None, d, bn), lambda j, i, te, tf, tr, nu: (te[i], 0, j)),
                 pl.BlockSpec((None, d, bn), lambda j, i, te, tf, tr, nu: (te[i], 0, nfb + j))]
    grid_spec = pltpu.PrefetchScalarGridSpec(
        num_scalar_prefetch=4,
        grid=(nfb, r // bm),
        in_specs=in_specs,
        out_specs=pl.BlockSpec((bm, bn), lambda j, i, te, tf, tr, nu: (i, j)),
        scratch_shapes=[pltpu.VMEM((d, bn), BF16), pltpu.VMEM((d, bn), BF16)],
    )
    return pl.pallas_call(
        functools.partial(_ffn_gu_body, norm=norm),
        grid_spec=grid_spec,
        out_shape=jax.ShapeDtypeStruct((r, f), BF16),
        compiler_params=_params("arbitrary", "arbitrary"),
        name="ffn_gate_up",
    )(*plan, *args, w_gu, w_gu)


def _ffn_dn_body(te_ref, tf_ref, tr_ref, nu_ref, x_ref, w_ref, *rest, residual):
    if residual:
        h_ref, o_ref, wbf_ref = rest
    else:
        o_ref, wbf_ref = rest
    i = pl.program_id(1)
    rows = tr_ref[i]

    @pl.when((rows > 0) & (tf_ref[i] == 1))
    def _():
        wbf_ref[...] = w_ref[...].astype(BF16)

    def compute(n):
        y = jnp.dot(x_ref[0:n, :], wbf_ref[...], preferred_element_type=F32)
        if residual:
            y = h_ref[0:n, :] + y
        o_ref[0:n, :] = y

    _for_tile_rows(rows, x_ref.shape[0], compute, o_ref)


def ffn_down(x, w_dn, plan, bm, h=None, bn=512):
    r, f = x.shape
    d = w_dn.shape[-1]
    row = lambda j, i, te, tf, tr, nu: (jnp.minimum(i, nu[0] - 1), 0)
    tile = lambda j, i, te, tf, tr, nu: (i, j)
    in_specs = [pl.BlockSpec((bm, f), row),
                pl.BlockSpec((None, f, bn), lambda j, i, te, tf, tr, nu: (te[i], 0, j))]
    args = [x, w_dn]
    if h is not None:
        in_specs.append(pl.BlockSpec((bm, bn), tile))
        args.append(h)
    grid_spec = pltpu.PrefetchScalarGridSpec(
        num_scalar_prefetch=4,
        grid=(d // bn, r // bm),
        in_specs=in_specs,
        out_specs=pl.BlockSpec((bm, bn), tile),
        scratch_shapes=[pltpu.VMEM((f, bn), BF16)],
    )
    return pl.pallas_call(
        functools.partial(_ffn_dn_body, residual=h is not None),
        grid_spec=grid_spec,
        out_shape=jax.ShapeDtypeStruct((r, d), F32),
        compiler_params=_params("arbitrary", "arbitrary"),
        name="ffn_down",
    )(*plan, *args)


def _router_body(h_ref, g_ref, w_ref, b_ref, xn_ref, meta_ref, cnt_ref, carry_ref, *, tm):
    @pl.when(pl.program_id(0) == 0)
    def _():
        carry_ref[...] = jnp.zeros(carry_ref.shape, F32)

    h = h_ref[...]
    xn = h * lax.rsqrt(jnp.mean(h * h, axis=-1, keepdims=True) + RMS_EPS) * g_ref[...]
    xn_ref[...] = xn.astype(xn_ref.dtype)
    lane = lax.broadcasted_iota(jnp.int32, (tm, LANES), 1)
    logits = jnp.dot(xn, w_ref[...], precision=lax.Precision.HIGHEST,
                     preferred_element_type=F32) + b_ref[...]
    neg = jnp.float32(-jnp.inf)
    logits = jnp.where(lane < N_EXPERTS, logits, neg)
    m1 = jnp.max(logits, axis=-1, keepdims=True)
    i1 = jnp.min(jnp.where(logits == m1, lane, LANES), axis=-1, keepdims=True)
    rest = jnp.where(lane == i1, neg, logits)
    m2 = jnp.max(rest, axis=-1, keepdims=True)
    i2 = jnp.min(jnp.where(rest == m2, lane, LANES), axis=-1, keepdims=True)
    e2 = jnp.exp(m2 - m1)
    g1 = 1.0 / (1.0 + e2)
    g2 = e2 / (1.0 + e2)
    hot1 = lane == i1
    hot2 = lane == i2
    hot = (hot1 | hot2).astype(BF16)
    row = lax.broadcasted_iota(jnp.int32, (tm, tm), 0)
    col = lax.broadcasted_iota(jnp.int32, (tm, tm), 1)
    before = (col < row).astype(BF16)
    carry = carry_ref[0:1, :]
    rank = jnp.dot(before, hot, preferred_element_type=F32) + carry
    r1 = jnp.sum(jnp.where(hot1, rank, 0.0), axis=-1, keepdims=True)
    r2 = jnp.sum(jnp.where(hot2, rank, 0.0), axis=-1, keepdims=True)
    total = carry + jnp.sum(hot.astype(F32), axis=0, keepdims=True)
    carry_ref[...] = jnp.broadcast_to(total, carry_ref.shape)
    cnt_ref[...] = jnp.broadcast_to(total, cnt_ref.shape)
    meta = jnp.where(lane == 0, i1.astype(F32), 0.0)
    meta = jnp.where(lane == 1, i2.astype(F32), meta)
    meta = jnp.where(lane == 2, g1, meta)
    meta = jnp.where(lane == 3, g2, meta)
    meta = jnp.where(lane == 4, r1, meta)
    meta = jnp.where(lane == 5, r2, meta)
    meta_ref[...] = meta


def router(h, gains, layer, w_pad, b_pad, tm=512):
    t, d = h.shape
    return pl.pallas_call(
        functools.partial(_router_body, tm=tm),
        grid=(t // tm,),
        in_specs=[pl.BlockSpec((tm, d), lambda i: (i, 0)),
                  pl.BlockSpec((None, 1, d), lambda i: (layer, 0, 0)),
                  pl.BlockSpec((d, LANES), lambda i: (0, 0)),
                  pl.BlockSpec((1, LANES), lambda i: (0, 0))],
        out_specs=[pl.BlockSpec((tm, d), lambda i: (i, 0)),
                   pl.BlockSpec((tm, LANES), lambda i: (i, 0)),
                   pl.BlockSpec((SUBLANES, LANES), lambda i: (0, 0))],
        out_shape=[jax.ShapeDtypeStruct((t, d), BF16),
                   jax.ShapeDtypeStruct((t, LANES), F32),
                   jax.ShapeDtypeStruct((SUBLANES, LANES), F32)],
        scratch_shapes=[pltpu.VMEM((SUBLANES, LANES), F32)],
        compiler_params=_params("arbitrary"),
        name="router",
    )(h, gains, w_pad, b_pad)


def _dispatch_body(pos_ref, x_ref, init_ref, o_ref, sem, *, tm):
    del init_ref
    base = pl.program_id(0) * (TOP_K * tm)

    def row_copy(r, k):
        p = pos_ref[base + TOP_K * r + k]
        return pltpu.make_async_copy(x_ref.at[r], o_ref.at[p], sem)

    def start(r, carry):
        for k in range(TOP_K):
            row_copy(r, k).start()
        return carry

    def wait(r, carry):
        for k in range(TOP_K):
            row_copy(r, k).wait()
        return carry

    lax.fori_loop(0, tm, start, 0)
    lax.fori_loop(0, tm, wait, 0)


def dispatch(xn, pos, rows, tm=512):
    t, d = xn.shape
    slab = (d // LANES, LANES)
    grid_spec = pltpu.PrefetchScalarGridSpec(
        num_scalar_prefetch=1,
        grid=(t // tm,),
        in_specs=[pl.BlockSpec((tm,) + slab, lambda i, pos: (i, 0, 0)),
                  pl.BlockSpec(memory_space=pl.ANY)],
        out_specs=pl.BlockSpec(memory_space=pl.ANY),
        scratch_shapes=[pltpu.SemaphoreType.DMA(())],
    )
    out = pl.pallas_call(
        functools.partial(_dispatch_body, tm=tm),
        grid_spec=grid_spec,
        out_shape=jax.ShapeDtypeStruct((rows,) + slab, xn.dtype),
        input_output_aliases={2: 0},
        compiler_params=_params("arbitrary"),
        name="moe_dispatch",
    )(pos, xn.reshape((t,) + slab), jnp.zeros((rows,) + slab, xn.dtype))
    return out.reshape(rows, d)


def _combine_body(pos_ref, y_ref, meta_ref, h_ref, o_ref, buf_ref, sem, *, tm):
    base = pl.program_id(0) * (TOP_K * tm)

    def row_copy(r, k):
        p = pos_ref[base + TOP_K * r + k]
        return pltpu.make_async_copy(y_ref.at[pl.ds(p, 1)], buf_ref.at[k, pl.ds(r, 1)], sem)

    def start(r, carry):
        for k in range(TOP_K):
            row_copy(r, k).start()
        return carry

    def wait(r, carry):
        for k in range(TOP_K):
            row_copy(r, k).wait()
        return carry

    lax.fori_loop(0, tm, start, 0)
    lax.fori_loop(0, tm, wait, 0)
    meta = meta_ref[...]
    o_ref[...] = h_ref[...] + meta[:, 2:3] * buf_ref[0] + meta[:, 3:4] * buf_ref[1]


def combine(y, pos, meta, h, tm=256):
    t, d = h.shape
    grid_spec = pltpu.PrefetchScalarGridSpec(
        num_scalar_prefetch=1,
        grid=(t // tm,),
        in_specs=[pl.BlockSpec(memory_space=pl.ANY),
                  pl.BlockSpec((tm, LANES), lambda i, pos: (i, 0)),
                  pl.BlockSpec((tm, d), lambda i, pos: (i, 0))],
        out_specs=pl.BlockSpec((tm, d), lambda i, pos: (i, 0)),
        scratch_shapes=[pltpu.VMEM((TOP_K, tm, d), F32), pltpu.SemaphoreType.DMA(())],
    )
    return pl.pallas_call(
        functools.partial(_combine_body, tm=tm),
        grid_spec=grid_spec,
        out_shape=jax.ShapeDtypeStruct((t, d), F32),
        compiler_params=_params("arbitrary"),
        name="moe_combine",
    )(pos, y, meta, h)


def moe_ffn(h, gains, layer, router_w, router_b, w_gu, w_dn, moe_layer, bm=512):
    t, d = h.shape
    e = N_EXPERTS
    w_pad = jnp.pad(router_w[moe_layer], ((0, 0), (0, LANES - e)))
    b_pad = jnp.pad(router_b[moe_layer], (0, LANES - e)).reshape(1, LANES)
    xn, meta, counts = router(h, gains, layer, w_pad, b_pad)
    counts = counts[0, :e].astype(jnp.int32)
    tiles = (counts + bm - 1) // bm
    tile_end = jnp.cumsum(tiles)
    group_start = (tile_end - tiles) * bm
    n_tiles = (TOP_K * t + e * (bm - 1)) // bm
    n_used = tile_end[-1:]
    tile_ids = jnp.arange(n_tiles, dtype=jnp.int32)
    tile_expert = jnp.sum(tile_ids[:, None] >= tile_end[None, :], axis=1).astype(jnp.int32)
    tile_expert = jnp.minimum(tile_expert, tile_expert[n_used[0] - 1])
    tile_first = jnp.concatenate([jnp.ones((1,), jnp.int32),
                                  (tile_expert[1:] != tile_expert[:-1]).astype(jnp.int32)])
    tile_rows = jnp.clip(counts[tile_expert] - (tile_ids * bm - group_start[tile_expert]), 0, bm)
    tile_rows = jnp.where(tile_ids < n_used[0], tile_rows, 0).astype(jnp.int32)
    sel = meta[:, 0:TOP_K].astype(jnp.int32)
    rank = meta[:, 4:4 + TOP_K].astype(jnp.int32)
    pos = (group_start[sel] + rank).reshape(-1)
    x_sorted = dispatch(xn, pos, n_tiles * bm)
    plan = (tile_expert + moe_layer * e, tile_first, tile_rows, n_used)
    mid = ffn_gate_up(x_sorted, w_gu, plan, bm)
    y = ffn_down(mid, w_dn, plan, bm)
    return combine(y, pos, meta, h)


def dense_ffn(h, gains, layer, w_gu, w_dn, dense_layer, bm_up=1024, bm_down=512):
    t = h.shape[0]

    def one_group(bm):
        n_tiles = t // bm
        return (jnp.full((n_tiles,), dense_layer, jnp.int32),
                jnp.zeros((n_tiles,), jnp.int32).at[0].set(1),
                jnp.full((n_tiles,), bm, jnp.int32),
                jnp.full((1,), n_tiles, jnp.int32))

    mid = ffn_gate_up(h, w_gu, one_group(bm_up), bm_up, gains=gains, layer=layer)
    return ffn_down(mid, w_dn, one_group(bm_down), bm_down, h=h)


def _ple_body(p_ref, x_ref, g_ref, wp_ref, wg_ref, h_ref, o_ref, wpbf_ref, wgbf_ref):
    @pl.when(pl.program_id(1) == 0)
    def _():
        wpbf_ref[...] = wp_ref[...].astype(BF16)
        wgbf_ref[...] = wg_ref[...].astype(BF16)

    e = jnp.dot(p_ref[...].astype(BF16), wpbf_ref[...], preferred_element_type=F32)
    xg, rstd = _scaled_rows(x_ref[...], g_ref[...])
    gate = jax.nn.sigmoid(rstd * jnp.dot(xg, wgbf_ref[...], preferred_element_type=F32))
    o_ref[...] = h_ref[...] + e * gate


def ple(h, gains, p, ple_w, gate_w, layer, bm=512, bn=1024):
    t, d = h.shape
    pd = p.shape[-1]
    return pl.pallas_call(
        _ple_body,
        grid=(d // bn, t // bm),
        in_specs=[pl.BlockSpec((None, bm, pd), lambda j, i: (layer, i, 0)),
                  pl.BlockSpec((bm, d), lambda j, i: (i, 0)),
                  pl.BlockSpec((None, 1, d), lambda j, i: (layer, 0, 0)),
                  pl.BlockSpec((None, pd, bn), lambda j, i: (layer, 0, j)),
                  pl.BlockSpec((None, d, bn), lambda j, i: (layer, 0, j)),
                  pl.BlockSpec((bm, bn), lambda j, i: (i, j))],
        out_specs=pl.BlockSpec((bm, bn), lambda j, i: (i, j)),
        out_shape=jax.ShapeDtypeStruct((t, d), F32),
        scratch_shapes=[pltpu.VMEM((pd, bn), BF16), pltpu.VMEM((d, bn), BF16)],
        compiler_params=_params("arbitrary", "arbitrary"),
        name="ple",
    )(p, h, gains, ple_w, gate_w, h)


def kernel(x, p, mix_norm, w_in, conv_w, conv_b, conv_ln_g, conv_ln_b, hg_lb_logits, hg_norm, w_out, ffn_norm, ffn_w_gu, ffn_w_dn, router_w, router_b, moe_w_gu, moe_w_dn, ple_w, ple_norm, ple_gate_w, final_norm):
    batch, seq, d = x.shape
    depth = w_in.shape[0]
    t = batch * seq
    d_conv = conv_w.shape[-1]
    heads = (w_out.shape[1] - d_conv) // HG_DV
    col_hg = 2 * d_conv

    h = x.reshape(t, d)
    p2 = p.reshape(depth, t, p.shape[-1])
    vec3 = lambda a: a.reshape(a.shape[0], 1, a.shape[-1])
    mix_g, ffn_g, ple_g = vec3(mix_norm), vec3(ffn_norm), vec3(ple_norm)
    cb, lng, lnb, gn = vec3(conv_b), vec3(conv_ln_g), vec3(conv_ln_b), vec3(hg_norm)
    moe_gu = moe_w_gu.reshape((-1,) + moe_w_gu.shape[2:])
    moe_dn = moe_w_dn.reshape((-1,) + moe_w_dn.shape[2:])

    for l in range(depth):
        z = proj_in(h, mix_g, w_in, l)
        yc = conv_group(z, conv_w, cb, lng, lnb, l, batch, seq, d_conv)
        yh = hgrn_group(z, hg_lb_logits, gn, l, batch, seq, heads, col_hg)
        h = proj_out(yc, yh, w_out, h, l)
        if l % 2 == 0:
            h = dense_ffn(h, ffn_g, l, ffn_w_gu, ffn_w_dn, l // 2)
        else:
            h = moe_ffn(h, ffn_g, l, router_w, router_b, moe_gu, moe_dn, l // 2)
        h = ple(h, ple_g, p2, ple_w, ple_gate_w, l)
    out = rmsnorm(h, final_norm.reshape(1, 1, d), 0, F32)
    return out.reshape(batch, seq, d)
```

```python
import functools

import jax
import jax.numpy as jnp
from jax import lax
from jax.experimental import pallas as pl
from jax.experimental.pallas import tpu as pltpu

F32 = jnp.float32
BF16 = jnp.bfloat16

RMS_EPS = 1e-6
LN_EPS = 1e-5
CONV_WIDTH = 31
HG_DK = 128
HG_DV = 128
CHUNK = 64
N_EXPERTS = 8
TOP_K = 2

LANES = 128
SUBLANES = 8
HGRN_UNROLL = 8
VMEM_LIMIT = 56 * 1024 * 1024

CONV_HALO = 32
CONV_STRIP = 32


def _params(*sem):
    return pltpu.CompilerParams(dimension_semantics=sem, vmem_limit_bytes=VMEM_LIMIT)


def _rmsnorm_body(x_ref, g_ref, o_ref):
    x = x_ref[...]
    ms = jnp.mean(x * x, axis=-1, keepdims=True)
    o_ref[...] = (x * lax.rsqrt(ms + RMS_EPS) * g_ref[...]).astype(o_ref.dtype)


def rmsnorm(h, gains, layer, out_dtype, bm=512):
    t, d = h.shape
    return pl.pallas_call(
        _rmsnorm_body,
        grid=(t // bm,),
        in_specs=[pl.BlockSpec((bm, d), lambda i: (i, 0)),
                  pl.BlockSpec((None, 1, d), lambda i: (layer, 0, 0))],
        out_specs=pl.BlockSpec((bm, d), lambda i: (i, 0)),
        out_shape=jax.ShapeDtypeStruct((t, d), out_dtype),
        compiler_params=_params("parallel"),
        name="rmsnorm",
    )(h, gains)


def _scaled_rows(x, gain):
    rstd = lax.rsqrt(jnp.mean(x * x, axis=-1, keepdims=True) + RMS_EPS)
    return (x * gain).astype(BF16), rstd


def _proj_in_body(x_ref, g_ref, w_ref, o_ref, wbf_ref):
    @pl.when(pl.program_id(1) == 0)
    def _():
        wbf_ref[...] = w_ref[...].astype(BF16)

    xg, rstd = _scaled_rows(x_ref[...], g_ref[...])
    o_ref[...] = rstd * jnp.dot(xg, wbf_ref[...], preferred_element_type=F32)


def proj_in(h, gains, w_in, layer, bm=512, bn=1536):
    t, d = h.shape
    n = w_in.shape[-1]
    return pl.pallas_call(
        _proj_in_body,
        grid=(n // bn, t // bm),
        in_specs=[pl.BlockSpec((bm, d), lambda j, i: (i, 0)),
                  pl.BlockSpec((None, 1, d), lambda j, i: (layer, 0, 0)),
                  pl.BlockSpec((None, d, bn), lambda j, i: (layer, 0, j))],
        out_specs=pl.BlockSpec((bm, bn), lambda j, i: (i, j)),
        out_shape=jax.ShapeDtypeStruct((t, n), F32),
        scratch_shapes=[pltpu.VMEM((d, bn), BF16)],
        compiler_params=_params("arbitrary", "arbitrary"),
        name="proj_in",
    )(h, gains, w_in)


def _conv_body(val_ref, gate_ref, w_ref, b_ref, lng_ref, lnb_ref, o_ref, u_ref, *, ts):
    s = pl.program_id(1)

    @pl.when(s == 0)
    def _():
        u_ref[0:CONV_HALO, :] = jnp.zeros((CONV_HALO, u_ref.shape[1]), F32)
        u_ref[CONV_HALO + ts:, :] = jnp.zeros((8, u_ref.shape[1]), F32)

    @pl.when(s > 0)
    def _():
        u_ref[0:CONV_HALO, :] = u_ref[ts:ts + CONV_HALO, :]

    u_ref[CONV_HALO:CONV_HALO + ts, :] = val_ref[...] * jax.nn.sigmoid(gate_ref[...])

    bias = b_ref[...]
    lng = lng_ref[...]
    lnb = lnb_ref[...]
    first_tap = CONV_HALO - (CONV_WIDTH - 1)

    n_win = CONV_STRIP + CONV_HALO + SUBLANES

    def strip(r, carry):
        r0 = pl.multiple_of(r * CONV_STRIP, CONV_STRIP)
        win = u_ref[pl.ds(r0, n_win), :]
        acc = jnp.zeros((CONV_STRIP, u_ref.shape[1]), F32)
        for sh in range(SUBLANES):
            shifted = win if sh == 0 else pltpu.roll(win, n_win - sh, axis=0)
            for a in range((CONV_HALO + SUBLANES) // SUBLANES):
                j = SUBLANES * a + sh - first_tap
                if 0 <= j < CONV_WIDTH:
                    acc = acc + w_ref[j:j + 1, :] * shifted[SUBLANES * a:SUBLANES * a + CONV_STRIP, :]
        c = acc + bias
        mu = jnp.mean(c, axis=-1, keepdims=True)
        cc = c - mu
        var = jnp.mean(cc * cc, axis=-1, keepdims=True)
        y = cc * lax.rsqrt(var + LN_EPS) * lng + lnb
        o_ref[pl.ds(r0, CONV_STRIP), :] = (y * jax.nn.sigmoid(y)).astype(o_ref.dtype)
        return carry

    lax.fori_loop(0, ts // CONV_STRIP, strip, 0)


def conv_group(z, conv_w, conv_b, ln_g, ln_b, layer, batch, seq, d_conv, ts=256):
    t = z.shape[0]
    nst = seq // ts
    vec = lambda: pl.BlockSpec((None, 1, d_conv), lambda b, s: (layer, 0, 0))
    return pl.pallas_call(
        functools.partial(_conv_body, ts=ts),
        grid=(batch, nst),
        in_specs=[pl.BlockSpec((ts, d_conv), lambda b, s: (b * nst + s, 0)),
                  pl.BlockSpec((ts, d_conv), lambda b, s: (b * nst + s, 1)),
                  pl.BlockSpec((None, CONV_WIDTH, d_conv), lambda b, s: (layer, 0, 0)),
                  vec(), vec(), vec()],
        out_specs=pl.BlockSpec((ts, d_conv), lambda b, s: (b * nst + s, 0)),
        out_shape=jax.ShapeDtypeStruct((t, d_conv), BF16),
        scratch_shapes=[pltpu.VMEM((CONV_HALO + ts + 8, d_conv), F32)],
        compiler_params=_params("arbitrary", "arbitrary"),
        name="conv_group",
    )(z, z, conv_w, conv_b, ln_g, ln_b)


_HALF_BLOCKS = (32, 16, 8, 4, 2, 1)


def _hgrn_body(lbl_ref, gn_ref, q_ref, f_ref, i_ref, g_ref, o_ref, b_scr, *, layer, n_chunks):
    lg = lbl_ref[...]
    ex = jnp.exp(lg - jnp.max(lg, axis=0, keepdims=True))
    sm = ex / jnp.sum(ex, axis=0, keepdims=True)
    lb = jnp.zeros((1, HG_DK), F32)
    for l in range(1, layer + 1):
        lb = lb + sm[l:l + 1, :]
    gn = gn_ref[...]

    c = CHUNK
    row = lax.broadcasted_iota(jnp.int32, (c, c), 0)
    col = lax.broadcasted_iota(jnp.int32, (c, c), 1)
    tril = (col <= row).astype(BF16)
    level = jnp.where(row == col, 0, -1)
    for lv, hb in enumerate(_HALF_BLOCKS):
        level = jnp.where((row // (2 * hb) == col // (2 * hb))
                          & (row % (2 * hb) >= hb) & (col % (2 * hb) < hb), lv + 1, level)
    sub = lax.broadcasted_iota(jnp.int32, (SUBLANES, HG_DK), 0)
    row_k = lax.broadcasted_iota(jnp.int32, (c, HG_DK), 0)
    nt = (((1,), (1,)), ((), ()))
    tn = (((0,), (0,)), ((), ()))

    def reference_rows(b, b_ref, hb):
        if hb == 1:
            return jnp.where(row_k % 2 == 0, b, pltpu.roll(b, 1, axis=0))
        bcast = lambda r: jnp.broadcast_to(b_ref[pl.ds(r, 1), :], (SUBLANES, HG_DK))
        groups = []
        for j in range(c // SUBLANES):
            r = SUBLANES * j
            if hb == 2:
                groups.append(jnp.where(sub < 4, bcast(r + 1), bcast(r + 5)))
            else:
                groups.append(bcast((r // (2 * hb)) * (2 * hb) + hb - 1))
        return jnp.concatenate(groups, axis=0)

    def chunk(ci, state_t, b_ref):
        r0 = pl.multiple_of(ci * c, c)
        q = q_ref[pl.ds(r0, c), :]
        f = lb + (1.0 - lb) * jax.nn.sigmoid(f_ref[pl.ds(r0, c), :])
        lf = jnp.log2(f)
        k = 1.0 - f
        v = i_ref[pl.ds(r0, c), :].astype(BF16)
        hi = lf.astype(BF16)
        rem = lf - hi.astype(F32)
        mid = rem.astype(BF16)
        lo = (rem - mid.astype(F32)).astype(BF16)
        b = (jnp.dot(tril, hi, preferred_element_type=F32)
             + jnp.dot(tril, mid, preferred_element_type=F32)
             + jnp.dot(tril, lo, preferred_element_type=F32))
        b_ref[...] = b
        scores = jnp.where(level == 0, lax.dot_general(q.astype(BF16), k.astype(BF16), nt,
                                                       preferred_element_type=F32), 0.0)
        for lv, hb in enumerate(_HALF_BLOCKS):
            e = jnp.exp2(-jnp.abs(b - reference_rows(b, b_ref, hb)))
            s_lv = lax.dot_general((q * e).astype(BF16), (k * e).astype(BF16), nt,
                                   preferred_element_type=F32)
            scores = jnp.where(level == lv + 1, s_lv, scores)
        o = (lax.dot_general((q * jnp.exp2(b)).astype(BF16), state_t.astype(BF16), nt,
                             preferred_element_type=F32)
             + jnp.dot(scores.astype(BF16), v, preferred_element_type=F32))
        b_last = b[c - 1:c, :]
        k_dec = (k * jnp.exp2(b_last - b)).astype(BF16)
        new_state_t = jnp.exp2(b_last) * state_t + lax.dot_general(
            v, k_dec, tn, preferred_element_type=F32)
        o = o * lax.rsqrt(jnp.mean(o * o, axis=-1, keepdims=True) + RMS_EPS) * gn
        g = g_ref[pl.ds(r0, c), :]
        o_ref[pl.ds(r0, c), :] = (o * (g * jax.nn.sigmoid(g))).astype(o_ref.dtype)
        return new_state_t

    def chunk_group(gi, state_t):
        for u in range(HGRN_UNROLL):
            state_t = chunk(gi * HGRN_UNROLL + u, state_t, b_scr.at[u])
        return state_t

    lax.fori_loop(0, n_chunks // HGRN_UNROLL, chunk_group, jnp.zeros((HG_DV, HG_DK), F32))


def hgrn_group(z, lb_logits, hg_norm, layer, batch, seq, heads, col0):
    t = z.shape[0]
    depth = lb_logits.shape[0]
    c0 = col0 // LANES
    zspec = lambda off: pl.BlockSpec((seq, LANES), lambda b, h: (b, c0 + off * heads + h))
    return pl.pallas_call(
        functools.partial(_hgrn_body, layer=layer, n_chunks=seq // CHUNK),
        grid=(batch, heads),
        in_specs=[pl.BlockSpec((depth, HG_DK), lambda b, h: (0, h)),
                  pl.BlockSpec((None, 1, HG_DV), lambda b, h: (layer, 0, 0)),
                  zspec(0), zspec(1), zspec(2), zspec(3)],
        out_specs=pl.BlockSpec((seq, HG_DV), lambda b, h: (b, h)),
        out_shape=jax.ShapeDtypeStruct((t, heads * HG_DV), BF16),
        scratch_shapes=[pltpu.VMEM((HGRN_UNROLL, CHUNK, HG_DK), F32)],
        compiler_params=_params("parallel", "parallel"),
        name="hgrn_group",
    )(lb_logits, hg_norm, z, z, z, z)


def _proj_out_body(yc_ref, yh_ref, w_ref, h_ref, o_ref, wbf_ref, *, d_conv):
    @pl.when(pl.program_id(1) == 0)
    def _():
        wbf_ref[...] = w_ref[...].astype(BF16)

    y = (jnp.dot(yc_ref[...], wbf_ref[0:d_conv, :], preferred_element_type=F32)
         + jnp.dot(yh_ref[...], wbf_ref[d_conv:, :], preferred_element_type=F32))
    o_ref[...] = h_ref[...] + y


def proj_out(yc, yh, w_out, h, layer, bm=1024, bn=1024):
    t, d_conv = yc.shape
    d_hg = yh.shape[1]
    d = h.shape[1]
    return pl.pallas_call(
        functools.partial(_proj_out_body, d_conv=d_conv),
        grid=(d // bn, t // bm),
        in_specs=[pl.BlockSpec((bm, d_conv), lambda j, i: (i, 0)),
                  pl.BlockSpec((bm, d_hg), lambda j, i: (i, 0)),
                  pl.BlockSpec((None, d_conv + d_hg, bn), lambda j, i: (layer, 0, j)),
                  pl.BlockSpec((bm, bn), lambda j, i: (i, j))],
        out_specs=pl.BlockSpec((bm, bn), lambda j, i: (i, j)),
        out_shape=jax.ShapeDtypeStruct((t, d), F32),
        scratch_shapes=[pltpu.VMEM((d_conv + d_hg, bn), BF16)],
        compiler_params=_params("arbitrary", "arbitrary"),
        name="proj_out",
    )(yc, yh, w_out, h)


def _for_tile_rows(rows, bm, parts, compute, o_ref):
    step = bm // parts
    for part in range(1, parts + 1):
        n = part * step

        @pl.when((rows > n - step) & (rows <= n))
        def _(n=n):
            compute(n)
            if n < bm:
                o_ref[n:, :] = jnp.zeros((bm - n, o_ref.shape[1]), o_ref.dtype)

    @pl.when(rows == 0)
    def _():
        o_ref[...] = jnp.zeros(o_ref.shape, o_ref.dtype)


def _ffn_gu_body(te_ref, tf_ref, tr_ref, nu_ref, x_ref, *rest, norm, parts):
    if norm:
        g_ref, wg_ref, wu_ref, o_ref, wgbf_ref, wubf_ref = rest
    else:
        wg_ref, wu_ref, o_ref, wgbf_ref, wubf_ref = rest
    i = pl.program_id(1)
    rows = tr_ref[i]

    @pl.when((rows > 0) & (tf_ref[i] == 1))
    def _():
        wgbf_ref[...] = wg_ref[...].astype(BF16)
        wubf_ref[...] = wu_ref[...].astype(BF16)

    def compute(n):
        if norm:
            x, rstd = _scaled_rows(x_ref[0:n, :], g_ref[...])
        else:
            x = x_ref[0:n, :]
        a = jnp.dot(x, wgbf_ref[...], preferred_element_type=F32)
        b = jnp.dot(x, wubf_ref[...], preferred_element_type=F32)
        if norm:
            a = rstd * a
            b = rstd * b
        o_ref[0:n, :] = (a * jax.nn.sigmoid(a) * b).astype(o_ref.dtype)

    _for_tile_rows(rows, x_ref.shape[0], parts, compute, o_ref)


def ffn_gate_up(x, w_gu, plan, bm, bn=512, gains=None, layer=0, parts=1):
    r, d = x.shape
    f = w_gu.shape[-1] // 2
    nfb = f // bn
    norm = gains is not None
    row = lambda j, i, te, tf, tr, nu: (jnp.minimum(i, nu[0] - 1), 0)
    in_specs = [pl.BlockSpec((bm, d), row)]
    args = [x]
    if norm:
        in_specs.append(pl.BlockSpec((None, 1, d), lambda j, i, te, tf, tr, nu: (layer, 0, 0)))
        args.append(gains)
    in_specs += [pl.BlockSpec((None, d, bn), lambda j, i, te, tf, tr, nu: (te[i], 0, j)),
                 pl.BlockSpec((None, d, bn), lambda j, i, te, tf, tr, nu: (te[i], 0, nfb + j))]
    grid_spec = pltpu.PrefetchScalarGridSpec(
        num_scalar_prefetch=4,
        grid=(nfb, r // bm),
        in_specs=in_specs,
        out_specs=pl.BlockSpec((bm, bn), lambda j, i, te, tf, tr, nu: (i, j)),
        scratch_shapes=[pltpu.VMEM((d, bn), BF16), pltpu.VMEM((d, bn), BF16)],
    )
    return pl.pallas_call(
        functools.partial(_ffn_gu_body, norm=norm, parts=parts),
        grid_spec=grid_spec,
        out_shape=jax.ShapeDtypeStruct((r, f), BF16),
        compiler_params=_params("arbitrary", "arbitrary"),
        name="ffn_gate_up",
    )(*plan, *args, w_gu, w_gu)


def _ffn_dn_body(te_ref, tf_ref, tr_ref, nu_ref, x_ref, w_ref, *rest, residual, parts):
    if residual:
        h_ref, o_ref, wbf_ref = rest
    else:
        o_ref, wbf_ref = rest
    i = pl.program_id(1)
    rows = tr_ref[i]

    @pl.when((rows > 0) & (tf_ref[i] == 1))
    def _():
        wbf_ref[...] = w_ref[...].astype(BF16)

    def compute(n):
        y = jnp.dot(x_ref[0:n, :], wbf_ref[...], preferred_element_type=F32)
        if residual:
            y = h_ref[0:n, :] + y
        o_ref[0:n, :] = y

    _for_tile_rows(rows, x_ref.shape[0], parts, compute, o_ref)


def ffn_down(x, w_dn, plan, bm, h=None, bn=512, parts=1):
    r, f = x.shape
    d = w_dn.shape[-1]
    row = lambda j, i, te, tf, tr, nu: (jnp.minimum(i, nu[0] - 1), 0)
    tile = lambda j, i, te, tf, tr, nu: (i, j)
    in_specs = [pl.BlockSpec((bm, f), row),
                pl.BlockSpec((None, f, bn), lambda j, i, te, tf, tr, nu: (te[i], 0, j))]
    args = [x, w_dn]
    if h is not None:
        in_specs.append(pl.BlockSpec((bm, bn), tile))
        args.append(h)
    grid_spec = pltpu.PrefetchScalarGridSpec(
        num_scalar_prefetch=4,
        grid=(d // bn, r // bm),
        in_specs=in_specs,
        out_specs=pl.BlockSpec((bm, bn), tile),
        scratch_shapes=[pltpu.VMEM((f, bn), BF16)],
    )
    return pl.pallas_call(
        functools.partial(_ffn_dn_body, residual=h is not None, parts=parts),
        grid_spec=grid_spec,
        out_shape=jax.ShapeDtypeStruct((r, d), F32),
        compiler_params=_params("arbitrary", "arbitrary"),
        name="ffn_down",
    )(*plan, *args)


def _router_body(h_ref, g_ref, w_ref, b_ref, xn_ref, meta_ref, cnt_ref, carry_ref, *, tm):
    @pl.when(pl.program_id(0) == 0)
    def _():
        carry_ref[...] = jnp.zeros(carry_ref.shape, F32)

    h = h_ref[...]
    xn = h * lax.rsqrt(jnp.mean(h * h, axis=-1, keepdims=True) + RMS_EPS) * g_ref[...]
    xn_ref[...] = xn.astype(xn_ref.dtype)
    lane = lax.broadcasted_iota(jnp.int32, (tm, LANES), 1)
    logits = jnp.dot(xn, w_ref[...], precision=lax.Precision.HIGHEST,
                     preferred_element_type=F32) + b_ref[...]
    neg = jnp.float32(-jnp.inf)
    logits = jnp.where(lane < N_EXPERTS, logits, neg)
    m1 = jnp.max(logits, axis=-1, keepdims=True)
    i1 = jnp.min(jnp.where(logits == m1, lane, LANES), axis=-1, keepdims=True)
    rest = jnp.where(lane == i1, neg, logits)
    m2 = jnp.max(rest, axis=-1, keepdims=True)
    i2 = jnp.min(jnp.where(rest == m2, lane, LANES), axis=-1, keepdims=True)
    e2 = jnp.exp(m2 - m1)
    g1 = 1.0 / (1.0 + e2)
    g2 = e2 / (1.0 + e2)
    hot1 = lane == i1
    hot2 = lane == i2
    hot = (hot1 | hot2).astype(BF16)
    row = lax.broadcasted_iota(jnp.int32, (tm, tm), 0)
    col = lax.broadcasted_iota(jnp.int32, (tm, tm), 1)
    before = (col < row).astype(BF16)
    carry = carry_ref[0:1, :]
    rank = jnp.dot(before, hot, preferred_element_type=F32) + carry
    r1 = jnp.sum(jnp.where(hot1, rank, 0.0), axis=-1, keepdims=True)
    r2 = jnp.sum(jnp.where(hot2, rank, 0.0), axis=-1, keepdims=True)
    total = carry + jnp.sum(hot.astype(F32), axis=0, keepdims=True)
    carry_ref[...] = jnp.broadcast_to(total, carry_ref.shape)
    cnt_ref[...] = jnp.broadcast_to(total, cnt_ref.shape)
    meta = jnp.where(lane == 0, i1.astype(F32), 0.0)
    meta = jnp.where(lane == 1, i2.astype(F32), meta)
    meta = jnp.where(lane == 2, g1, meta)
    meta = jnp.where(lane == 3, g2, meta)
    meta = jnp.where(lane == 4, r1, meta)
    meta = jnp.where(lane == 5, r2, meta)
    meta_ref[...] = meta


def router(h, gains, layer, w_pad, b_pad, tm=512):
    t, d = h.shape
    return pl.pallas_call(
        functools.partial(_router_body, tm=tm),
        grid=(t // tm,),
        in_specs=[pl.BlockSpec((tm, d), lambda i: (i, 0)),
                  pl.BlockSpec((None, 1, d), lambda i: (layer, 0, 0)),
                  pl.BlockSpec((d, LANES), lambda i: (0, 0)),
                  pl.BlockSpec((1, LANES), lambda i: (0, 0))],
        out_specs=[pl.BlockSpec((tm, d), lambda i: (i, 0)),
                   pl.BlockSpec((tm, LANES), lambda i: (i, 0)),
                   pl.BlockSpec((SUBLANES, LANES), lambda i: (0, 0))],
        out_shape=[jax.ShapeDtypeStruct((t, d), BF16),
                   jax.ShapeDtypeStruct((t, LANES), F32),
                   jax.ShapeDtypeStruct((SUBLANES, LANES), F32)],
        scratch_shapes=[pltpu.VMEM((SUBLANES, LANES), F32)],
        compiler_params=_params("arbitrary"),
        name="router",
    )(h, gains, w_pad, b_pad)


def _dispatch_body(pos_ref, x_ref, init_ref, o_ref, sem, *, tm):
    del init_ref
    base = pl.program_id(0) * (TOP_K * tm)

    def row_copy(r, k):
        p = pos_ref[base + TOP_K * r + k]
        return pltpu.make_async_copy(x_ref.at[r], o_ref.at[p], sem)

    def start(r, carry):
        for k in range(TOP_K):
            row_copy(r, k).start()
        return carry

    def wait(r, carry):
        for k in range(TOP_K):
            row_copy(r, k).wait()
        return carry

    lax.fori_loop(0, tm, start, 0)
    lax.fori_loop(0, tm, wait, 0)


def dispatch(xn, pos, rows, tm=512):
    t, d = xn.shape
    slab = (d // LANES, LANES)
    grid_spec = pltpu.PrefetchScalarGridSpec(
        num_scalar_prefetch=1,
        grid=(t // tm,),
        in_specs=[pl.BlockSpec((tm,) + slab, lambda i, pos: (i, 0, 0)),
                  pl.BlockSpec(memory_space=pl.ANY)],
        out_specs=pl.BlockSpec(memory_space=pl.ANY),
        scratch_shapes=[pltpu.SemaphoreType.DMA(())],
    )
    out = pl.pallas_call(
        functools.partial(_dispatch_body, tm=tm),
        grid_spec=grid_spec,
        out_shape=jax.ShapeDtypeStruct((rows,) + slab, xn.dtype),
        input_output_aliases={2: 0},
        compiler_params=_params("arbitrary"),
        name="moe_dispatch",
    )(pos, xn.reshape((t,) + slab), jnp.zeros((rows,) + slab, xn.dtype))
    return out.reshape(rows, d)


def _combine_body(pos_ref, y_ref, meta_ref, h_ref, o_ref, buf_ref, sem, *, tm):
    base = pl.program_id(0) * (TOP_K * tm)

    def row_copy(r, k):
        p = pos_ref[base + TOP_K * r + k]
        return pltpu.make_async_copy(y_ref.at[pl.ds(p, 1)], buf_ref.at[k, pl.ds(r, 1)], sem)

    def start(r, carry):
        for k in range(TOP_K):
            row_copy(r, k).start()
        return carry

    def wait(r, carry):
        for k in range(TOP_K):
            row_copy(r, k).wait()
        return carry

    lax.fori_loop(0, tm, start, 0)
    lax.fori_loop(0, tm, wait, 0)
    meta = meta_ref[...]
    o_ref[...] = h_ref[...] + meta[:, 2:3] * buf_ref[0] + meta[:, 3:4] * buf_ref[1]


def combine(y, pos, meta, h, tm=256):
    t, d = h.shape
    grid_spec = pltpu.PrefetchScalarGridSpec(
        num_scalar_prefetch=1,
        grid=(t // tm,),
        in_specs=[pl.BlockSpec(memory_space=pl.ANY),
                  pl.BlockSpec((tm, LANES), lambda i, pos: (i, 0)),
                  pl.BlockSpec((tm, d), lambda i, pos: (i, 0))],
        out_specs=pl.BlockSpec((tm, d), lambda i, pos: (i, 0)),
        scratch_shapes=[pltpu.VMEM((TOP_K, tm, d), F32), pltpu.SemaphoreType.DMA(())],
    )
    return pl.pallas_call(
        functools.partial(_combine_body, tm=tm),
        grid_spec=grid_spec,
        out_shape=jax.ShapeDtypeStruct((t, d), F32),
        compiler_params=_params("arbitrary"),
        name="moe_combine",
    )(pos, y, meta, h)


def _group_plan(counts, group_start, n_rows, bm, first_expert):
    n_tiles = n_rows // bm
    group_tiles = (counts + bm - 1) // bm
    first_tile = group_start // bm
    tile_ids = jnp.arange(n_tiles, dtype=jnp.int32)
    n_used = (first_tile[-1] + group_tiles[-1]).reshape(1)
    tile_expert = (jnp.sum(tile_ids[:, None] >= first_tile[None, :], axis=1) - 1).astype(jnp.int32)
    tile_first = jnp.concatenate([jnp.ones((1,), jnp.int32),
                                  (tile_expert[1:] != tile_expert[:-1]).astype(jnp.int32)])
    tile_rows = jnp.clip(counts[tile_expert] - (tile_ids - first_tile[tile_expert]) * bm, 0, bm)
    return (tile_expert + first_expert, tile_first, tile_rows.astype(jnp.int32), n_used.astype(jnp.int32))


def moe_ffn(h, gains, layer, router_w, router_b, w_gu, w_dn, moe_layer, bm_up=1024, bm_down=512):
    t, d = h.shape
    e = N_EXPERTS
    w_pad = jnp.pad(router_w[moe_layer], ((0, 0), (0, LANES - e)))
    b_pad = jnp.pad(router_b[moe_layer], (0, LANES - e)).reshape(1, LANES)
    xn, meta, counts = router(h, gains, layer, w_pad, b_pad)
    counts = counts[0, :e].astype(jnp.int32)
    group_rows = ((counts + bm_up - 1) // bm_up) * bm_up
    group_start = jnp.cumsum(group_rows) - group_rows
    n_rows = ((TOP_K * t + e * (bm_up - 1)) // bm_up) * bm_up
    sel = meta[:, 0:TOP_K].astype(jnp.int32)
    rank = meta[:, 4:4 + TOP_K].astype(jnp.int32)
    pos = (group_start[sel] + rank).reshape(-1)
    x_sorted = dispatch(xn, pos, n_rows)
    mid = ffn_gate_up(x_sorted, w_gu, _group_plan(counts, group_start, n_rows, bm_up, moe_layer * e),
                      bm_up, parts=4)
    y = ffn_down(mid, w_dn, _group_plan(counts, group_start, n_rows, bm_down, moe_layer * e),
                 bm_down, parts=2)
    return combine(y, pos, meta, h)


def dense_ffn(h, gains, layer, w_gu, w_dn, dense_layer, bm_up=1024, bm_down=512):
    t = h.shape[0]

    def one_group(bm):
        n_tiles = t // bm
        return (jnp.full((n_tiles,), dense_layer, jnp.int32),
                jnp.zeros((n_tiles,), jnp.int32).at[0].set(1),
                jnp.full((n_tiles,), bm, jnp.int32),
                jnp.full((1,), n_tiles, jnp.int32))

    mid = ffn_gate_up(h, w_gu, one_group(bm_up), bm_up, gains=gains, layer=layer)
    return ffn_down(mid, w_dn, one_group(bm_down), bm_down, h=h)


def _ple_body(p_ref, x_ref, g_ref, wp_ref, wg_ref, h_ref, o_ref, wpbf_ref, wgbf_ref):
    @pl.when(pl.program_id(1) == 0)
    def _():
        wpbf_ref[...] = wp_ref[...].astype(BF16)
        wgbf_ref[...] = wg_ref[...].astype(BF16)

    e = jnp.dot(p_ref[...].astype(BF16), wpbf_ref[...], preferred_element_type=F32)
    xg, rstd = _scaled_rows(x_ref[...], g_ref[...])
    gate = jax.nn.sigmoid(rstd * jnp.dot(xg, wgbf_ref[...], preferred_element_type=F32))
    o_ref[...] = h_ref[...] + e * gate


def ple(h, gains, p, ple_w, gate_w, layer, bm=512, bn=1024):
    t, d = h.shape
    pd = p.shape[-1]
    return pl.pallas_call(
        _ple_body,
        grid=(d // bn, t // bm),
        in_specs=[pl.BlockSpec((None, bm, pd), lambda j, i: (layer, i, 0)),
                  pl.BlockSpec((bm, d), lambda j, i: (i, 0)),
                  pl.BlockSpec((None, 1, d), lambda j, i: (layer, 0, 0)),
                  pl.BlockSpec((None, pd, bn), lambda j, i: (layer, 0, j)),
                  pl.BlockSpec((None, d, bn), lambda j, i: (layer, 0, j)),
                  pl.BlockSpec((bm, bn), lambda j, i: (i, j))],
        out_specs=pl.BlockSpec((bm, bn), lambda j, i: (i, j)),
        out_shape=jax.ShapeDtypeStruct((t, d), F32),
        scratch_shapes=[pltpu.VMEM((pd, bn), BF16), pltpu.VMEM((d, bn), BF16)],
        compiler_params=_params("arbitrary", "arbitrary"),
        name="ple",
    )(p, h, gains, ple_w, gate_w, h)


def kernel(x, p, mix_norm, w_in, conv_w, conv_b, conv_ln_g, conv_ln_b, hg_lb_logits, hg_norm, w_out, ffn_norm, ffn_w_gu, ffn_w_dn, router_w, router_b, moe_w_gu, moe_w_dn, ple_w, ple_norm, ple_gate_w, final_norm):
    batch, seq, d = x.shape
    depth = w_in.shape[0]
    t = batch * seq
    d_conv = conv_w.shape[-1]
    heads = (w_out.shape[1] - d_conv) // HG_DV
    col_hg = 2 * d_conv

    h = x.reshape(t, d)
    p2 = p.reshape(depth, t, p.shape[-1])
    vec3 = lambda a: a.reshape(a.shape[0], 1, a.shape[-1])
    mix_g, ffn_g, ple_g = vec3(mix_norm), vec3(ffn_norm), vec3(ple_norm)
    cb, lng, lnb, gn = vec3(conv_b), vec3(conv_ln_g), vec3(conv_ln_b), vec3(hg_norm)
    moe_gu = moe_w_gu.reshape((-1,) + moe_w_gu.shape[2:])
    moe_dn = moe_w_dn.reshape((-1,) + moe_w_dn.shape[2:])

    for l in range(depth):
        z = proj_in(h, mix_g, w_in, l)
        yc = conv_group(z, conv_w, cb, lng, lnb, l, batch, seq, d_conv)
        yh = hgrn_group(z, hg_lb_logits, gn, l, batch, seq, heads, col_hg)
        h = proj_out(yc, yh, w_out, h, l)
        if l % 2 == 0:
            h = dense_ffn(h, ffn_g, l, ffn_w_gu, ffn_w_dn, l // 2)
        else:
            h = moe_ffn(h, ffn_g, l, router_w, router_b, moe_gu, moe_dn, l // 2)
        h = ple(h, ple_g, p2, ple_w, ple_gate_w, l)
    out = rmsnorm(h, final_norm.reshape(1, 1, d), 0, F32)
    return out.reshape(batch, seq, d)
```

```python
import functools

import jax
import jax.numpy as jnp
from jax import lax
from jax.experimental import pallas as pl
from jax.experimental.pallas import tpu as pltpu

F32 = jnp.float32
BF16 = jnp.bfloat16

RMS_EPS = 1e-6
LN_EPS = 1e-5
CONV_WIDTH = 31
HG_DK = 128
HG_DV = 128
CHUNK = 64
N_EXPERTS = 8
TOP_K = 2

LANES = 128
SUBLANES = 8
HGRN_UNROLL = 8
VMEM_LIMIT = 56 * 1024 * 1024

CONV_HALO = 32
CONV_STRIP = 32


def _params(*sem):
    return pltpu.CompilerParams(dimension_semantics=sem, vmem_limit_bytes=VMEM_LIMIT)


def _rmsnorm_body(x_ref, g_ref, o_ref):
    x = x_ref[...]
    ms = jnp.mean(x * x, axis=-1, keepdims=True)
    o_ref[...] = (x * lax.rsqrt(ms + RMS_EPS) * g_ref[...]).astype(o_ref.dtype)


def rmsnorm(h, gains, layer, out_dtype, bm=512):
    t, d = h.shape
    return pl.pallas_call(
        _rmsnorm_body,
        grid=(t // bm,),
        in_specs=[pl.BlockSpec((bm, d), lambda i: (i, 0)),
                  pl.BlockSpec((None, 1, d), lambda i: (layer, 0, 0))],
        out_specs=pl.BlockSpec((bm, d), lambda i: (i, 0)),
        out_shape=jax.ShapeDtypeStruct((t, d), out_dtype),
        compiler_params=_params("parallel"),
        name="rmsnorm",
    )(h, gains)


def _scaled_rows(x, gain):
    rstd = lax.rsqrt(jnp.mean(x * x, axis=-1, keepdims=True) + RMS_EPS)
    return (x * gain).astype(BF16), rstd


def _proj_in_body(x_ref, g_ref, w_ref, o_ref, wbf_ref):
    @pl.when(pl.program_id(1) == 0)
    def _():
        wbf_ref[...] = w_ref[...].astype(BF16)

    xg, rstd = _scaled_rows(x_ref[...], g_ref[...])
    o_ref[...] = rstd * jnp.dot(xg, wbf_ref[...], preferred_element_type=F32)


def proj_in(h, gains, w_in, layer, bm=512, bn=1536):
    t, d = h.shape
    n = w_in.shape[-1]
    return pl.pallas_call(
        _proj_in_body,
        grid=(n // bn, t // bm),
        in_specs=[pl.BlockSpec((bm, d), lambda j, i: (i, 0)),
                  pl.BlockSpec((None, 1, d), lambda j, i: (layer, 0, 0)),
                  pl.BlockSpec((None, d, bn), lambda j, i: (layer, 0, j))],
        out_specs=pl.BlockSpec((bm, bn), lambda j, i: (i, j)),
        out_shape=jax.ShapeDtypeStruct((t, n), F32),
        scratch_shapes=[pltpu.VMEM((d, bn), BF16)],
        compiler_params=_params("arbitrary", "arbitrary"),
        name="proj_in",
    )(h, gains, w_in)


def _conv_body(val_ref, gate_ref, w_ref, b_ref, lng_ref, lnb_ref, o_ref, u_ref, *, ts):
    s = pl.program_id(1)

    @pl.when(s == 0)
    def _():
        u_ref[0:CONV_HALO, :] = jnp.zeros((CONV_HALO, u_ref.shape[1]), F32)
        u_ref[CONV_HALO + ts:, :] = jnp.zeros((8, u_ref.shape[1]), F32)

    @pl.when(s > 0)
    def _():
        u_ref[0:CONV_HALO, :] = u_ref[ts:ts + CONV_HALO, :]

    u_ref[CONV_HALO:CONV_HALO + ts, :] = val_ref[...] * jax.nn.sigmoid(gate_ref[...])

    bias = b_ref[...]
    lng = lng_ref[...]
    lnb = lnb_ref[...]
    first_tap = CONV_HALO - (CONV_WIDTH - 1)

    n_win = CONV_STRIP + CONV_HALO + SUBLANES

    def strip(r, carry):
        r0 = pl.multiple_of(r * CONV_STRIP, CONV_STRIP)
        win = u_ref[pl.ds(r0, n_win), :]
        acc = jnp.zeros((CONV_STRIP, u_ref.shape[1]), F32)
        for sh in range(SUBLANES):
            shifted = win if sh == 0 else pltpu.roll(win, n_win - sh, axis=0)
            for a in range((CONV_HALO + SUBLANES) // SUBLANES):
                j = SUBLANES * a + sh - first_tap
                if 0 <= j < CONV_WIDTH:
                    acc = acc + w_ref[j:j + 1, :] * shifted[SUBLANES * a:SUBLANES * a + CONV_STRIP, :]
        c = acc + bias
        mu = jnp.mean(c, axis=-1, keepdims=True)
        cc = c - mu
        var = jnp.mean(cc * cc, axis=-1, keepdims=True)
        y = cc * lax.rsqrt(var + LN_EPS) * lng + lnb
        o_ref[pl.ds(r0, CONV_STRIP), :] = (y * jax.nn.sigmoid(y)).astype(o_ref.dtype)
        return carry

    lax.fori_loop(0, ts // CONV_STRIP, strip, 0)


def conv_group(z, conv_w, conv_b, ln_g, ln_b, layer, batch, seq, d_conv, ts=256):
    t = z.shape[0]
    nst = seq // ts
    vec = lambda: pl.BlockSpec((None, 1, d_conv), lambda b, s: (layer, 0, 0))
    return pl.pallas_call(
        functools.partial(_conv_body, ts=ts),
        grid=(batch, nst),
        in_specs=[pl.BlockSpec((ts, d_conv), lambda b, s: (b * nst + s, 0)),
                  pl.BlockSpec((ts, d_conv), lambda b, s: (b * nst + s, 1)),
                  pl.BlockSpec((None, CONV_WIDTH, d_conv), lambda b, s: (layer, 0, 0)),
                  vec(), vec(), vec()],
        out_specs=pl.BlockSpec((ts, d_conv), lambda b, s: (b * nst + s, 0)),
        out_shape=jax.ShapeDtypeStruct((t, d_conv), BF16),
        scratch_shapes=[pltpu.VMEM((CONV_HALO + ts + 8, d_conv), F32)],
        compiler_params=_params("arbitrary", "arbitrary"),
        name="conv_group",
    )(z, z, conv_w, conv_b, ln_g, ln_b)


_HALF_BLOCKS = (32, 16, 8, 4, 2, 1)


def _hgrn_body(lbl_ref, gn_ref, q_ref, f_ref, i_ref, g_ref, o_ref, b_scr, *, layer, n_chunks):
    lg = lbl_ref[...]
    ex = jnp.exp(lg - jnp.max(lg, axis=0, keepdims=True))
    sm = ex / jnp.sum(ex, axis=0, keepdims=True)
    lb = jnp.zeros((1, HG_DK), F32)
    for l in range(1, layer + 1):
        lb = lb + sm[l:l + 1, :]
    gn = gn_ref[...]

    c = CHUNK
    row = lax.broadcasted_iota(jnp.int32, (c, c), 0)
    col = lax.broadcasted_iota(jnp.int32, (c, c), 1)
    tril = (col <= row).astype(BF16)
    level = jnp.where(row == col, 0, -1)
    for lv, hb in enumerate(_HALF_BLOCKS):
        level = jnp.where((row // (2 * hb) == col // (2 * hb))
                          & (row % (2 * hb) >= hb) & (col % (2 * hb) < hb), lv + 1, level)
    sub = lax.broadcasted_iota(jnp.int32, (SUBLANES, HG_DK), 0)
    row_k = lax.broadcasted_iota(jnp.int32, (c, HG_DK), 0)
    nt = (((1,), (1,)), ((), ()))
    tn = (((0,), (0,)), ((), ()))

    def reference_rows(b, b_ref, hb):
        if hb == 1:
            return jnp.where(row_k % 2 == 0, b, pltpu.roll(b, 1, axis=0))
        bcast = lambda r: jnp.broadcast_to(b_ref[pl.ds(r, 1), :], (SUBLANES, HG_DK))
        groups = []
        for j in range(c // SUBLANES):
            r = SUBLANES * j
            if hb == 2:
                groups.append(jnp.where(sub < 4, bcast(r + 1), bcast(r + 5)))
            else:
                groups.append(bcast((r // (2 * hb)) * (2 * hb) + hb - 1))
        return jnp.concatenate(groups, axis=0)

    def chunk(ci, state_t, b_ref):
        r0 = pl.multiple_of(ci * c, c)
        q = q_ref[pl.ds(r0, c), :]
        f = lb + (1.0 - lb) * jax.nn.sigmoid(f_ref[pl.ds(r0, c), :])
        lf = jnp.log2(f)
        k = 1.0 - f
        v = i_ref[pl.ds(r0, c), :].astype(BF16)
        hi = lf.astype(BF16)
        rem = lf - hi.astype(F32)
        mid = rem.astype(BF16)
        lo = (rem - mid.astype(F32)).astype(BF16)
        b = (jnp.dot(tril, hi, preferred_element_type=F32)
             + jnp.dot(tril, mid, preferred_element_type=F32)
             + jnp.dot(tril, lo, preferred_element_type=F32))
        b_ref[...] = b
        scores = jnp.where(level == 0, lax.dot_general(q.astype(BF16), k.astype(BF16), nt,
                                                       preferred_element_type=F32), 0.0)
        for lv, hb in enumerate(_HALF_BLOCKS):
            e = jnp.exp2(-jnp.abs(b - reference_rows(b, b_ref, hb)))
            s_lv = lax.dot_general((q * e).astype(BF16), (k * e).astype(BF16), nt,
                                   preferred_element_type=F32)
            scores = jnp.where(level == lv + 1, s_lv, scores)
        o = (lax.dot_general((q * jnp.exp2(b)).astype(BF16), state_t.astype(BF16), nt,
                             preferred_element_type=F32)
             + jnp.dot(scores.astype(BF16), v, preferred_element_type=F32))
        b_last = b[c - 1:c, :]
        k_dec = (k * jnp.exp2(b_last - b)).astype(BF16)
        new_state_t = jnp.exp2(b_last) * state_t + lax.dot_general(
            v, k_dec, tn, preferred_element_type=F32)
        o = o * lax.rsqrt(jnp.mean(o * o, axis=-1, keepdims=True) + RMS_EPS) * gn
        g = g_ref[pl.ds(r0, c), :]
        o_ref[pl.ds(r0, c), :] = (o * (g * jax.nn.sigmoid(g))).astype(o_ref.dtype)
        return new_state_t

    def chunk_group(gi, state_t):
        for u in range(HGRN_UNROLL):
            state_t = chunk(gi * HGRN_UNROLL + u, state_t, b_scr.at[u])
        return state_t

    lax.fori_loop(0, n_chunks // HGRN_UNROLL, chunk_group, jnp.zeros((HG_DV, HG_DK), F32))


def hgrn_group(z, lb_logits, hg_norm, layer, batch, seq, heads, col0):
    t = z.shape[0]
    depth = lb_logits.shape[0]
    c0 = col0 // LANES
    zspec = lambda off: pl.BlockSpec((seq, LANES), lambda b, h: (b, c0 + off * heads + h))
    return pl.pallas_call(
        functools.partial(_hgrn_body, layer=layer, n_chunks=seq // CHUNK),
        grid=(batch, heads),
        in_specs=[pl.BlockSpec((depth, HG_DK), lambda b, h: (0, h)),
                  pl.BlockSpec((None, 1, HG_DV), lambda b, h: (layer, 0, 0)),
                  zspec(0), zspec(1), zspec(2), zspec(3)],
        out_specs=pl.BlockSpec((seq, HG_DV), lambda b, h: (b, h)),
        out_shape=jax.ShapeDtypeStruct((t, heads * HG_DV), BF16),
        scratch_shapes=[pltpu.VMEM((HGRN_UNROLL, CHUNK, HG_DK), F32)],
        compiler_params=_params("parallel", "parallel"),
        name="hgrn_group",
    )(lb_logits, hg_norm, z, z, z, z)


def _proj_out_body(yc_ref, yh_ref, w_ref, h_ref, o_ref, wbf_ref, *, d_conv):
    @pl.when(pl.program_id(1) == 0)
    def _():
        wbf_ref[...] = w_ref[...].astype(BF16)

    y = (jnp.dot(yc_ref[...], wbf_ref[0:d_conv, :], preferred_element_type=F32)
         + jnp.dot(yh_ref[...], wbf_ref[d_conv:, :], preferred_element_type=F32))
    o_ref[...] = h_ref[...] + y


def proj_out(yc, yh, w_out, h, layer, bm=1024, bn=1024):
    t, d_conv = yc.shape
    d_hg = yh.shape[1]
    d = h.shape[1]
    return pl.pallas_call(
        functools.partial(_proj_out_body, d_conv=d_conv),
        grid=(d // bn, t // bm),
        in_specs=[pl.BlockSpec((bm, d_conv), lambda j, i: (i, 0)),
                  pl.BlockSpec((bm, d_hg), lambda j, i: (i, 0)),
                  pl.BlockSpec((None, d_conv + d_hg, bn), lambda j, i: (layer, 0, j)),
                  pl.BlockSpec((bm, bn), lambda j, i: (i, j))],
        out_specs=pl.BlockSpec((bm, bn), lambda j, i: (i, j)),
        out_shape=jax.ShapeDtypeStruct((t, d), F32),
        scratch_shapes=[pltpu.VMEM((d_conv + d_hg, bn), BF16)],
        compiler_params=_params("arbitrary", "arbitrary"),
        name="proj_out",
    )(yc, yh, w_out, h)


def _for_tile_rows(rows, bm, parts, compute, o_ref):
    step = bm // parts
    for part in range(1, parts + 1):
        n = part * step

        @pl.when((rows > n - step) & (rows <= n))
        def _(n=n):
            compute(n)
            if n < bm:
                o_ref[n:, :] = jnp.zeros((bm - n, o_ref.shape[1]), o_ref.dtype)

    @pl.when(rows == 0)
    def _():
        o_ref[...] = jnp.zeros(o_ref.shape, o_ref.dtype)


def _ffn_gu_body(te_ref, tf_ref, tr_ref, tx_ref, to_ref, x_ref, *rest, norm, parts):
    if norm:
        g_ref, wg_ref, wu_ref, o_ref, wgbf_ref, wubf_ref = rest
    else:
        wg_ref, wu_ref, o_ref, wgbf_ref, wubf_ref = rest
    i = pl.program_id(1)
    rows = tr_ref[i]

    @pl.when((rows > 0) & (tf_ref[i] == 1))
    def _():
        wgbf_ref[...] = wg_ref[...].astype(BF16)
        wubf_ref[...] = wu_ref[...].astype(BF16)

    def compute(n):
        if norm:
            x, rstd = _scaled_rows(x_ref[0:n, :], g_ref[...])
        else:
            x = x_ref[0:n, :]
        a = jnp.dot(x, wgbf_ref[...], preferred_element_type=F32)
        b = jnp.dot(x, wubf_ref[...], preferred_element_type=F32)
        if norm:
            a = rstd * a
            b = rstd * b
        o_ref[0:n, :] = (a * jax.nn.sigmoid(a) * b).astype(o_ref.dtype)

    _for_tile_rows(rows, x_ref.shape[0], parts, compute, o_ref)


def ffn_gate_up(x, w_gu, plan, bm, bn=512, gains=None, layer=0, parts=1):
    r, d = x.shape
    f = w_gu.shape[-1] // 2
    nfb = f // bn
    norm = gains is not None
    row = lambda j, i, te, tf, tr, tx, to: (tx[i], 0)
    in_specs = [pl.BlockSpec((bm, d), row)]
    args = [x]
    if norm:
        in_specs.append(pl.BlockSpec((None, 1, d), lambda j, i, te, tf, tr, tx, to: (layer, 0, 0)))
        args.append(gains)
    in_specs += [pl.BlockSpec((None, d, bn), lambda j, i, te, tf, tr, tx, to: (te[i], 0, j)),
                 pl.BlockSpec((None, d, bn), lambda j, i, te, tf, tr, tx, to: (te[i], 0, nfb + j))]
    grid_spec = pltpu.PrefetchScalarGridSpec(
        num_scalar_prefetch=5,
        grid=(nfb, r // bm),
        in_specs=in_specs,
        out_specs=pl.BlockSpec((bm, bn), lambda j, i, te, tf, tr, tx, to: (to[i], j)),
        scratch_shapes=[pltpu.VMEM((d, bn), BF16), pltpu.VMEM((d, bn), BF16)],
    )
    return pl.pallas_call(
        functools.partial(_ffn_gu_body, norm=norm, parts=parts),
        grid_spec=grid_spec,
        out_shape=jax.ShapeDtypeStruct((r, f), BF16),
        compiler_params=_params("arbitrary", "arbitrary"),
        name="ffn_gate_up",
    )(*plan, *args, w_gu, w_gu)


def _ffn_dn_body(te_ref, tf_ref, tr_ref, tx_ref, to_ref, x_ref, w_ref, *rest, residual, parts):
    if residual:
        h_ref, o_ref, wbf_ref = rest
    else:
        o_ref, wbf_ref = rest
    i = pl.program_id(1)
    rows = tr_ref[i]

    @pl.when((rows > 0) & (tf_ref[i] == 1))
    def _():
        wbf_ref[...] = w_ref[...].astype(BF16)

    def compute(n):
        y = jnp.dot(x_ref[0:n, :], wbf_ref[...], preferred_element_type=F32)
        if residual:
            y = h_ref[0:n, :] + y
        o_ref[0:n, :] = y

    _for_tile_rows(rows, x_ref.shape[0], parts, compute, o_ref)


def ffn_down(x, w_dn, plan, bm, h=None, bn=512, parts=1):
    r, f = x.shape
    d = w_dn.shape[-1]
    row = lambda j, i, te, tf, tr, tx, to: (tx[i], 0)
    tile = lambda j, i, te, tf, tr, tx, to: (to[i], j)
    in_specs = [pl.BlockSpec((bm, f), row),
                pl.BlockSpec((None, f, bn), lambda j, i, te, tf, tr, tx, to: (te[i], 0, j))]
    args = [x, w_dn]
    if h is not None:
        in_specs.append(pl.BlockSpec((bm, bn), tile))
        args.append(h)
    grid_spec = pltpu.PrefetchScalarGridSpec(
        num_scalar_prefetch=5,
        grid=(d // bn, r // bm),
        in_specs=in_specs,
        out_specs=pl.BlockSpec((bm, bn), tile),
        scratch_shapes=[pltpu.VMEM((f, bn), BF16)],
    )
    return pl.pallas_call(
        functools.partial(_ffn_dn_body, residual=h is not None, parts=parts),
        grid_spec=grid_spec,
        out_shape=jax.ShapeDtypeStruct((r, d), F32),
        compiler_params=_params("arbitrary", "arbitrary"),
        name="ffn_down",
    )(*plan, *args)


def _router_body(h_ref, g_ref, w_ref, b_ref, xn_ref, meta_ref, cnt_ref, carry_ref, *, tm):
    @pl.when(pl.program_id(0) == 0)
    def _():
        carry_ref[...] = jnp.zeros(carry_ref.shape, F32)

    h = h_ref[...]
    xn = h * lax.rsqrt(jnp.mean(h * h, axis=-1, keepdims=True) + RMS_EPS) * g_ref[...]
    xn_ref[...] = xn.astype(xn_ref.dtype)
    lane = lax.broadcasted_iota(jnp.int32, (tm, LANES), 1)
    w = w_ref[...]
    x_hi, w_hi = xn.astype(BF16), w.astype(BF16)
    x_lo, w_lo = (xn - x_hi.astype(F32)).astype(BF16), (w - w_hi.astype(F32)).astype(BF16)
    logits = (jnp.dot(x_hi, w_hi, preferred_element_type=F32)
              + (jnp.dot(x_hi, w_lo, preferred_element_type=F32)
                 + jnp.dot(x_lo, w_hi, preferred_element_type=F32))) + b_ref[...]
    neg = jnp.float32(-jnp.inf)
    logits = jnp.where(lane < N_EXPERTS, logits, neg)
    m1 = jnp.max(logits, axis=-1, keepdims=True)
    i1 = jnp.min(jnp.where(logits == m1, lane, LANES), axis=-1, keepdims=True)
    rest = jnp.where(lane == i1, neg, logits)
    m2 = jnp.max(rest, axis=-1, keepdims=True)
    i2 = jnp.min(jnp.where(rest == m2, lane, LANES), axis=-1, keepdims=True)
    e2 = jnp.exp(m2 - m1)
    g1 = 1.0 / (1.0 + e2)
    g2 = e2 / (1.0 + e2)
    hot1 = lane == i1
    hot2 = lane == i2
    hot = (hot1 | hot2).astype(BF16)
    row = lax.broadcasted_iota(jnp.int32, (tm, tm), 0)
    col = lax.broadcasted_iota(jnp.int32, (tm, tm), 1)
    before = (col < row).astype(BF16)
    carry = carry_ref[0:1, :]
    rank = jnp.dot(before, hot, preferred_element_type=F32) + carry
    r1 = jnp.sum(jnp.where(hot1, rank, 0.0), axis=-1, keepdims=True)
    r2 = jnp.sum(jnp.where(hot2, rank, 0.0), axis=-1, keepdims=True)
    total = carry + jnp.sum(hot.astype(F32), axis=0, keepdims=True)
    carry_ref[...] = jnp.broadcast_to(total, carry_ref.shape)
    cnt_ref[...] = jnp.broadcast_to(total, cnt_ref.shape)
    meta = jnp.where(lane == 0, i1.astype(F32), 0.0)
    meta = jnp.where(lane == 1, i2.astype(F32), meta)
    meta = jnp.where(lane == 2, g1, meta)
    meta = jnp.where(lane == 3, g2, meta)
    meta = jnp.where(lane == 4, r1, meta)
    meta = jnp.where(lane == 5, r2, meta)
    meta_ref[...] = meta


def router(h, gains, layer, w_pad, b_pad, tm=512):
    t, d = h.shape
    return pl.pallas_call(
        functools.partial(_router_body, tm=tm),
        grid=(t // tm,),
        in_specs=[pl.BlockSpec((tm, d), lambda i: (i, 0)),
                  pl.BlockSpec((None, 1, d), lambda i: (layer, 0, 0)),
                  pl.BlockSpec((d, LANES), lambda i: (0, 0)),
                  pl.BlockSpec((1, LANES), lambda i: (0, 0))],
        out_specs=[pl.BlockSpec((tm, d), lambda i: (i, 0)),
                   pl.BlockSpec((tm, LANES), lambda i: (i, 0)),
                   pl.BlockSpec((SUBLANES, LANES), lambda i: (0, 0))],
        out_shape=[jax.ShapeDtypeStruct((t, d), BF16),
                   jax.ShapeDtypeStruct((t, LANES), F32),
                   jax.ShapeDtypeStruct((SUBLANES, LANES), F32)],
        scratch_shapes=[pltpu.VMEM((SUBLANES, LANES), F32)],
        compiler_params=_params("arbitrary"),
        name="router",
    )(h, gains, w_pad, b_pad)


def _dispatch_body(pos_ref, x_ref, init_ref, o_ref, sem, *, tm):
    del init_ref
    base = pl.program_id(0) * (TOP_K * tm)

    def row_copy(r, k):
        p = pos_ref[base + TOP_K * r + k]
        return pltpu.make_async_copy(x_ref.at[r], o_ref.at[p], sem)

    def start(r, carry):
        for k in range(TOP_K):
            row_copy(r, k).start()
        return carry

    def wait(r, carry):
        for k in range(TOP_K):
            row_copy(r, k).wait()
        return carry

    lax.fori_loop(0, tm, start, 0)
    lax.fori_loop(0, tm, wait, 0)


def dispatch(xn, pos, rows, tm=512):
    t, d = xn.shape
    slab = (d // LANES, LANES)
    grid_spec = pltpu.PrefetchScalarGridSpec(
        num_scalar_prefetch=1,
        grid=(t // tm,),
        in_specs=[pl.BlockSpec((tm,) + slab, lambda i, pos: (i, 0, 0)),
                  pl.BlockSpec(memory_space=pl.ANY)],
        out_specs=pl.BlockSpec(memory_space=pl.ANY),
        scratch_shapes=[pltpu.SemaphoreType.DMA(())],
    )
    out = pl.pallas_call(
        functools.partial(_dispatch_body, tm=tm),
        grid_spec=grid_spec,
        out_shape=jax.ShapeDtypeStruct((rows,) + slab, xn.dtype),
        input_output_aliases={2: 0},
        compiler_params=_params("arbitrary"),
        name="moe_dispatch",
    )(pos, xn.reshape((t,) + slab), jnp.zeros((rows,) + slab, xn.dtype))
    return out.reshape(rows, d)


def _combine_body(pos_ref, y_ref, meta_ref, h_ref, o_ref, buf_ref, sem, *, tm):
    base = pl.program_id(0) * (TOP_K * tm)

    def row_copy(r, k):
        p = pos_ref[base + TOP_K * r + k]
        return pltpu.make_async_copy(y_ref.at[pl.ds(p, 1)], buf_ref.at[k, pl.ds(r, 1)], sem)

    def start(r, carry):
        for k in range(TOP_K):
            row_copy(r, k).start()
        return carry

    def wait(r, carry):
        for k in range(TOP_K):
            row_copy(r, k).wait()
        return carry

    lax.fori_loop(0, tm, start, 0)
    lax.fori_loop(0, tm, wait, 0)
    meta = meta_ref[...]
    o_ref[...] = h_ref[...] + meta[:, 2:3] * buf_ref[0] + meta[:, 3:4] * buf_ref[1]


def combine(y, pos, meta, h, tm=256):
    t, d = h.shape
    grid_spec = pltpu.PrefetchScalarGridSpec(
        num_scalar_prefetch=1,
        grid=(t // tm,),
        in_specs=[pl.BlockSpec(memory_space=pl.ANY),
                  pl.BlockSpec((tm, LANES), lambda i, pos: (i, 0)),
                  pl.BlockSpec((tm, d), lambda i, pos: (i, 0))],
        out_specs=pl.BlockSpec((tm, d), lambda i, pos: (i, 0)),
        scratch_shapes=[pltpu.VMEM((TOP_K, tm, d), F32), pltpu.SemaphoreType.DMA(())],
    )
    return pl.pallas_call(
        functools.partial(_combine_body, tm=tm),
        grid_spec=grid_spec,
        out_shape=jax.ShapeDtypeStruct((t, d), F32),
        compiler_params=_params("arbitrary"),
        name="moe_combine",
    )(pos, y, meta, h)


def _group_plan(counts, group_start, n_rows, bm, first_expert):
    n_experts = counts.shape[0]
    n_tiles = n_rows // bm
    first_tile = group_start // bm
    tile_ids = jnp.arange(n_tiles, dtype=jnp.int32)
    group = (jnp.sum(tile_ids[:, None] >= first_tile[None, :], axis=1) - 1).astype(jnp.int32)
    rows = jnp.clip(counts[group] - (tile_ids - first_tile[group]) * bm, 0, bm).astype(jnp.int32)
    block = jnp.where(rows > 0, group, (group + 1) % n_experts)
    order = jnp.argsort(block * 3 + (rows > 0) + (rows == bm), stable=True).astype(jnp.int32)
    rows, block = rows[order], block[order]
    busy = rows > 0
    last_busy = lax.cummax(jnp.where(busy, tile_ids, -1))
    last_busy = jnp.where(last_busy < 0, jnp.argmax(busy).astype(jnp.int32), last_busy)
    prev_busy = jnp.concatenate([last_busy[:1], last_busy[:-1]])
    first = busy & ((jnp.cumsum(busy) == 1) | (block != block[prev_busy]))
    return (block + first_expert, first.astype(jnp.int32), rows, order[last_busy], order)


def moe_ffn(h, gains, layer, router_w, router_b, w_gu, w_dn, moe_layer, bm_up=1024, bm_down=512):
    t, d = h.shape
    e = N_EXPERTS
    w_pad = jnp.pad(router_w[moe_layer], ((0, 0), (0, LANES - e)))
    b_pad = jnp.pad(router_b[moe_layer], (0, LANES - e)).reshape(1, LANES)
    xn, meta, counts = router(h, gains, layer, w_pad, b_pad)
    counts = counts[0, :e].astype(jnp.int32)
    group_rows = ((counts + bm_up - 1) // bm_up) * bm_up
    group_start = jnp.cumsum(group_rows) - group_rows
    n_rows = ((TOP_K * t + e * (bm_up - 1)) // bm_up) * bm_up
    sel = meta[:, 0:TOP_K].astype(jnp.int32)
    rank = meta[:, 4:4 + TOP_K].astype(jnp.int32)
    pos = (group_start[sel] + rank).reshape(-1)
    x_sorted = dispatch(xn, pos, n_rows)
    mid = ffn_gate_up(x_sorted, w_gu, _group_plan(counts, group_start, n_rows, bm_up, moe_layer * e),
                      bm_up, parts=4)
    y = ffn_down(mid, w_dn, _group_plan(counts, group_start, n_rows, bm_down, moe_layer * e),
                 bm_down, parts=2)
    return combine(y, pos, meta, h)


def dense_ffn(h, gains, layer, w_gu, w_dn, dense_layer, bm_up=1024, bm_down=512):
    t = h.shape[0]

    def one_group(bm):
        n_tiles = t // bm
        tile_ids = jnp.arange(n_tiles, dtype=jnp.int32)
        return (jnp.full((n_tiles,), dense_layer, jnp.int32),
                (tile_ids == 0).astype(jnp.int32),
                jnp.full((n_tiles,), bm, jnp.int32),
                tile_ids, tile_ids)

    mid = ffn_gate_up(h, w_gu, one_group(bm_up), bm_up, gains=gains, layer=layer)
    return ffn_down(mid, w_dn, one_group(bm_down), bm_down, h=h)


def _ple_body(p_ref, x_ref, g_ref, wp_ref, wg_ref, h_ref, o_ref, wpbf_ref, wgbf_ref):
    @pl.when(pl.program_id(1) == 0)
    def _():
        wpbf_ref[...] = wp_ref[...].astype(BF16)
        wgbf_ref[...] = wg_ref[...].astype(BF16)

    e = jnp.dot(p_ref[...].astype(BF16), wpbf_ref[...], preferred_element_type=F32)
    xg, rstd = _scaled_rows(x_ref[...], g_ref[...])
    gate = jax.nn.sigmoid(rstd * jnp.dot(xg, wgbf_ref[...], preferred_element_type=F32))
    o_ref[...] = h_ref[...] + e * gate


def ple(h, gains, p, ple_w, gate_w, layer, bm=512, bn=1024):
    t, d = h.shape
    pd = p.shape[-1]
    return pl.pallas_call(
        _ple_body,
        grid=(d // bn, t // bm),
        in_specs=[pl.BlockSpec((None, bm, pd), lambda j, i: (layer, i, 0)),
                  pl.BlockSpec((bm, d), lambda j, i: (i, 0)),
                  pl.BlockSpec((None, 1, d), lambda j, i: (layer, 0, 0)),
                  pl.BlockSpec((None, pd, bn), lambda j, i: (layer, 0, j)),
                  pl.BlockSpec((None, d, bn), lambda j, i: (layer, 0, j)),
                  pl.BlockSpec((bm, bn), lambda j, i: (i, j))],
        out_specs=pl.BlockSpec((bm, bn), lambda j, i: (i, j)),
        out_shape=jax.ShapeDtypeStruct((t, d), F32),
        scratch_shapes=[pltpu.VMEM((pd, bn), BF16), pltpu.VMEM((d, bn), BF16)],
        compiler_params=_params("arbitrary", "arbitrary"),
        name="ple",
    )(p, h, gains, ple_w, gate_w, h)


def kernel(x, p, mix_norm, w_in, conv_w, conv_b, conv_ln_g, conv_ln_b, hg_lb_logits, hg_norm, w_out, ffn_norm, ffn_w_gu, ffn_w_dn, router_w, router_b, moe_w_gu, moe_w_dn, ple_w, ple_norm, ple_gate_w, final_norm):
    batch, seq, d = x.shape
    depth = w_in.shape[0]
    t = batch * seq
    d_conv = conv_w.shape[-1]
    heads = (w_out.shape[1] - d_conv) // HG_DV
    col_hg = 2 * d_conv

    h = x.reshape(t, d)
    p2 = p.reshape(depth, t, p.shape[-1])
    vec3 = lambda a: a.reshape(a.shape[0], 1, a.shape[-1])
    mix_g, ffn_g, ple_g = vec3(mix_norm), vec3(ffn_norm), vec3(ple_norm)
    cb, lng, lnb, gn = vec3(conv_b), vec3(conv_ln_g), vec3(conv_ln_b), vec3(hg_norm)
    moe_gu = moe_w_gu.reshape((-1,) + moe_w_gu.shape[2:])
    moe_dn = moe_w_dn.reshape((-1,) + moe_w_dn.shape[2:])

    for l in range(depth):
        z = proj_in(h, mix_g, w_in, l)
        yc = conv_group(z, conv_w, cb, lng, lnb, l, batch, seq, d_conv)
        yh = hgrn_group(z, hg_lb_logits, gn, l, batch, seq, heads, col_hg)
        h = proj_out(yc, yh, w_out, h, l)
        if l % 2 == 0:
            h = dense_ffn(h, ffn_g, l, ffn_w_gu, ffn_w_dn, l // 2)
        else:
            h = moe_ffn(h, ffn_g, l, router_w, router_b, moe_gu, moe_dn, l // 2)
        h = ple(h, ple_g, p2, ple_w, ple_gate_w, l)
    out = rmsnorm(h, final_norm.reshape(1, 1, d), 0, F32)
    return out.reshape(batch, seq, d)
```

```python
import functools

import jax
import jax.numpy as jnp
from jax import lax
from jax.experimental import pallas as pl
from jax.experimental.pallas import tpu as pltpu

F32 = jnp.float32
BF16 = jnp.bfloat16

RMS_EPS = 1e-6
LN_EPS = 1e-5
CONV_WIDTH = 31
HG_DK = 128
HG_DV = 128
CHUNK = 64
N_EXPERTS = 8
TOP_K = 2

LANES = 128
SUBLANES = 8
HGRN_UNROLL = 8
VMEM_LIMIT = 56 * 1024 * 1024

CONV_HALO = 32
CONV_STRIP = 32


def _params(*sem):
    return pltpu.CompilerParams(dimension_semantics=sem, vmem_limit_bytes=VMEM_LIMIT)


def _rmsnorm_body(x_ref, g_ref, o_ref):
    x = x_ref[...]
    ms = jnp.mean(x * x, axis=-1, keepdims=True)
    o_ref[...] = (x * lax.rsqrt(ms + RMS_EPS) * g_ref[...]).astype(o_ref.dtype)


def rmsnorm(h, gains, layer, out_dtype, bm=512):
    t, d = h.shape
    return pl.pallas_call(
        _rmsnorm_body,
        grid=(t // bm,),
        in_specs=[pl.BlockSpec((bm, d), lambda i: (i, 0)),
                  pl.BlockSpec((None, 1, d), lambda i: (layer, 0, 0))],
        out_specs=pl.BlockSpec((bm, d), lambda i: (i, 0)),
        out_shape=jax.ShapeDtypeStruct((t, d), out_dtype),
        compiler_params=_params("parallel"),
        name="rmsnorm",
    )(h, gains)


def _scaled_rows(x, gain):
    rstd = lax.rsqrt(jnp.mean(x * x, axis=-1, keepdims=True) + RMS_EPS)
    return (x * gain).astype(BF16), rstd


def _proj_in_body(x_ref, g_ref, w_ref, o_ref, wbf_ref):
    @pl.when(pl.program_id(1) == 0)
    def _():
        wbf_ref[...] = w_ref[...].astype(BF16)

    xg, rstd = _scaled_rows(x_ref[...], g_ref[...])
    o_ref[...] = rstd * jnp.dot(xg, wbf_ref[...], preferred_element_type=F32)


def proj_in(h, gains, w_in, layer, bm=512, bn=1536):
    t, d = h.shape
    n = w_in.shape[-1]
    return pl.pallas_call(
        _proj_in_body,
        grid=(n // bn, t // bm),
        in_specs=[pl.BlockSpec((bm, d), lambda j, i: (i, 0)),
                  pl.BlockSpec((None, 1, d), lambda j, i: (layer, 0, 0)),
                  pl.BlockSpec((None, d, bn), lambda j, i: (layer, 0, j))],
        out_specs=pl.BlockSpec((bm, bn), lambda j, i: (i, j)),
        out_shape=jax.ShapeDtypeStruct((t, n), F32),
        scratch_shapes=[pltpu.VMEM((d, bn), BF16)],
        compiler_params=_params("arbitrary", "arbitrary"),
        name="proj_in",
    )(h, gains, w_in)


def _conv_body(val_ref, gate_ref, w_ref, b_ref, lng_ref, lnb_ref, o_ref, u_ref, *, ts):
    s = pl.program_id(1)

    @pl.when(s == 0)
    def _():
        u_ref[0:CONV_HALO, :] = jnp.zeros((CONV_HALO, u_ref.shape[1]), F32)
        u_ref[CONV_HALO + ts:, :] = jnp.zeros((8, u_ref.shape[1]), F32)

    @pl.when(s > 0)
    def _():
        u_ref[0:CONV_HALO, :] = u_ref[ts:ts + CONV_HALO, :]

    u_ref[CONV_HALO:CONV_HALO + ts, :] = val_ref[...] * jax.nn.sigmoid(gate_ref[...])

    bias = b_ref[...]
    lng = lng_ref[...]
    lnb = lnb_ref[...]
    first_tap = CONV_HALO - (CONV_WIDTH - 1)

    n_win = CONV_STRIP + CONV_HALO + SUBLANES

    def strip(r, carry):
        r0 = pl.multiple_of(r * CONV_STRIP, CONV_STRIP)
        win = u_ref[pl.ds(r0, n_win), :]
        acc = jnp.zeros((CONV_STRIP, u_ref.shape[1]), F32)
        for sh in range(SUBLANES):
            shifted = win if sh == 0 else pltpu.roll(win, n_win - sh, axis=0)
            for a in range((CONV_HALO + SUBLANES) // SUBLANES):
                j = SUBLANES * a + sh - first_tap
                if 0 <= j < CONV_WIDTH:
                    acc = acc + w_ref[j:j + 1, :] * shifted[SUBLANES * a:SUBLANES * a + CONV_STRIP, :]
        c = acc + bias
        mu = jnp.mean(c, axis=-1, keepdims=True)
        cc = c - mu
        var = jnp.mean(cc * cc, axis=-1, keepdims=True)
        y = cc * lax.rsqrt(var + LN_EPS) * lng + lnb
        o_ref[pl.ds(r0, CONV_STRIP), :] = (y * jax.nn.sigmoid(y)).astype(o_ref.dtype)
        return carry

    lax.fori_loop(0, ts // CONV_STRIP, strip, 0)


def conv_group(z, conv_w, conv_b, ln_g, ln_b, layer, batch, seq, d_conv, ts=256):
    t = z.shape[0]
    nst = seq // ts
    vec = lambda: pl.BlockSpec((None, 1, d_conv), lambda b, s: (layer, 0, 0))
    return pl.pallas_call(
        functools.partial(_conv_body, ts=ts),
        grid=(batch, nst),
        in_specs=[pl.BlockSpec((ts, d_conv), lambda b, s: (b * nst + s, 0)),
                  pl.BlockSpec((ts, d_conv), lambda b, s: (b * nst + s, 1)),
                  pl.BlockSpec((None, CONV_WIDTH, d_conv), lambda b, s: (layer, 0, 0)),
                  vec(), vec(), vec()],
        out_specs=pl.BlockSpec((ts, d_conv), lambda b, s: (b * nst + s, 0)),
        out_shape=jax.ShapeDtypeStruct((t, d_conv), BF16),
        scratch_shapes=[pltpu.VMEM((CONV_HALO + ts + 8, d_conv), F32)],
        compiler_params=_params("arbitrary", "arbitrary"),
        name="conv_group",
    )(z, z, conv_w, conv_b, ln_g, ln_b)


_HALF_BLOCKS = (32, 16, 8, 4, 2, 1)


def _hgrn_body(lbl_ref, gn_ref, q_ref, f_ref, i_ref, g_ref, o_ref, b_scr, *, layer, n_chunks):
    lg = lbl_ref[...]
    ex = jnp.exp(lg - jnp.max(lg, axis=0, keepdims=True))
    sm = ex / jnp.sum(ex, axis=0, keepdims=True)
    lb = jnp.zeros((1, HG_DK), F32)
    for l in range(1, layer + 1):
        lb = lb + sm[l:l + 1, :]
    gn = gn_ref[...]

    c = CHUNK
    row = lax.broadcasted_iota(jnp.int32, (c, c), 0)
    col = lax.broadcasted_iota(jnp.int32, (c, c), 1)
    tril = (col <= row).astype(BF16)
    level = jnp.where(row == col, 0, -1)
    for lv, hb in enumerate(_HALF_BLOCKS):
        level = jnp.where((row // (2 * hb) == col // (2 * hb))
                          & (row % (2 * hb) >= hb) & (col % (2 * hb) < hb), lv + 1, level)
    sub = lax.broadcasted_iota(jnp.int32, (SUBLANES, HG_DK), 0)
    row_k = lax.broadcasted_iota(jnp.int32, (c, HG_DK), 0)
    nt = (((1,), (1,)), ((), ()))
    tn = (((0,), (0,)), ((), ()))

    def reference_rows(b, b_ref, hb):
        if hb == 1:
            return jnp.where(row_k % 2 == 0, b, pltpu.roll(b, 1, axis=0))
        bcast = lambda r: jnp.broadcast_to(b_ref[pl.ds(r, 1), :], (SUBLANES, HG_DK))
        groups = []
        for j in range(c // SUBLANES):
            r = SUBLANES * j
            if hb == 2:
                groups.append(jnp.where(sub < 4, bcast(r + 1), bcast(r + 5)))
            else:
                groups.append(bcast((r // (2 * hb)) * (2 * hb) + hb - 1))
        return jnp.concatenate(groups, axis=0)

    def chunk(ci, state_t, b_ref):
        r0 = pl.multiple_of(ci * c, c)
        q = q_ref[pl.ds(r0, c), :]
        f = lb + (1.0 - lb) * jax.nn.sigmoid(f_ref[pl.ds(r0, c), :])
        lf = jnp.log2(f)
        k = 1.0 - f
        v = i_ref[pl.ds(r0, c), :].astype(BF16)
        hi = lf.astype(BF16)
        rem = lf - hi.astype(F32)
        mid = rem.astype(BF16)
        lo = (rem - mid.astype(F32)).astype(BF16)
        b = (jnp.dot(tril, hi, preferred_element_type=F32)
             + jnp.dot(tril, mid, preferred_element_type=F32)
             + jnp.dot(tril, lo, preferred_element_type=F32))
        b_ref[...] = b
        scores = jnp.where(level == 0, lax.dot_general(q.astype(BF16), k.astype(BF16), nt,
                                                       preferred_element_type=F32), 0.0)
        for lv, hb in enumerate(_HALF_BLOCKS):
            e = jnp.exp2(-jnp.abs(b - reference_rows(b, b_ref, hb)))
            s_lv = lax.dot_general((q * e).astype(BF16), (k * e).astype(BF16), nt,
                                   preferred_element_type=F32)
            scores = jnp.where(level == lv + 1, s_lv, scores)
        o = (lax.dot_general((q * jnp.exp2(b)).astype(BF16), state_t.astype(BF16), nt,
                             preferred_element_type=F32)
             + jnp.dot(scores.astype(BF16), v, preferred_element_type=F32))
        b_last = b[c - 1:c, :]
        k_dec = (k * jnp.exp2(b_last - b)).astype(BF16)
        new_state_t = jnp.exp2(b_last) * state_t + lax.dot_general(
            v, k_dec, tn, preferred_element_type=F32)
        o = o * lax.rsqrt(jnp.mean(o * o, axis=-1, keepdims=True) + RMS_EPS) * gn
        g = g_ref[pl.ds(r0, c), :]
        o_ref[pl.ds(r0, c), :] = (o * (g * jax.nn.sigmoid(g))).astype(o_ref.dtype)
        return new_state_t

    def chunk_group(gi, state_t):
        for u in range(HGRN_UNROLL):
            state_t = chunk(gi * HGRN_UNROLL + u, state_t, b_scr.at[u])
        return state_t

    lax.fori_loop(0, n_chunks // HGRN_UNROLL, chunk_group, jnp.zeros((HG_DV, HG_DK), F32))


def hgrn_group(z, lb_logits, hg_norm, layer, batch, seq, heads, col0):
    t = z.shape[0]
    depth = lb_logits.shape[0]
    c0 = col0 // LANES
    zspec = lambda off: pl.BlockSpec((seq, LANES), lambda b, h: (b, c0 + off * heads + h))
    return pl.pallas_call(
        functools.partial(_hgrn_body, layer=layer, n_chunks=seq // CHUNK),
        grid=(batch, heads),
        in_specs=[pl.BlockSpec((depth, HG_DK), lambda b, h: (0, h)),
                  pl.BlockSpec((None, 1, HG_DV), lambda b, h: (layer, 0, 0)),
                  zspec(0), zspec(1), zspec(2), zspec(3)],
        out_specs=pl.BlockSpec((seq, HG_DV), lambda b, h: (b, h)),
        out_shape=jax.ShapeDtypeStruct((t, heads * HG_DV), BF16),
        scratch_shapes=[pltpu.VMEM((HGRN_UNROLL, CHUNK, HG_DK), F32)],
        compiler_params=_params("parallel", "parallel"),
        name="hgrn_group",
    )(lb_logits, hg_norm, z, z, z, z)


def _proj_out_body(yc_ref, yh_ref, w_ref, h_ref, o_ref, wbf_ref, *, d_conv):
    @pl.when(pl.program_id(1) == 0)
    def _():
        wbf_ref[...] = w_ref[...].astype(BF16)

    y = (jnp.dot(yc_ref[...], wbf_ref[0:d_conv, :], preferred_element_type=F32)
         + jnp.dot(yh_ref[...], wbf_ref[d_conv:, :], preferred_element_type=F32))
    o_ref[...] = h_ref[...] + y


def proj_out(yc, yh, w_out, h, layer, bm=1024, bn=1024):
    t, d_conv = yc.shape
    d_hg = yh.shape[1]
    d = h.shape[1]
    return pl.pallas_call(
        functools.partial(_proj_out_body, d_conv=d_conv),
        grid=(d // bn, t // bm),
        in_specs=[pl.BlockSpec((bm, d_conv), lambda j, i: (i, 0)),
                  pl.BlockSpec((bm, d_hg), lambda j, i: (i, 0)),
                  pl.BlockSpec((None, d_conv + d_hg, bn), lambda j, i: (layer, 0, j)),
                  pl.BlockSpec((bm, bn), lambda j, i: (i, j))],
        out_specs=pl.BlockSpec((bm, bn), lambda j, i: (i, j)),
        out_shape=jax.ShapeDtypeStruct((t, d), F32),
        scratch_shapes=[pltpu.VMEM((d_conv + d_hg, bn), BF16)],
        compiler_params=_params("arbitrary", "arbitrary"),
        name="proj_out",
    )(yc, yh, w_out, h)


def _weight_set_ahead(plan):
    te, tf = plan[0], plan[1]
    return jnp.where(tf == 1, te, jnp.concatenate([te[1:], te[-1:]]))


def _for_tile_rows(rows, bm, parts, compute, o_ref):
    step = bm // parts
    for part in range(1, parts + 1):
        n = part * step

        @pl.when((rows > n - step) & (rows <= n))
        def _(n=n):
            compute(n)
            if n < bm:
                o_ref[n:, :] = jnp.zeros((bm - n, o_ref.shape[1]), o_ref.dtype)

    @pl.when(rows == 0)
    def _():
        o_ref[...] = jnp.zeros(o_ref.shape, o_ref.dtype)


def _ffn_gu_body(te_ref, tf_ref, tr_ref, tx_ref, to_ref, ta_ref, x_ref, *rest, norm, parts):
    if norm:
        g_ref, wg_ref, wu_ref, o_ref, wgbf_ref, wubf_ref = rest
    else:
        wg_ref, wu_ref, o_ref, wgbf_ref, wubf_ref = rest
    i = pl.program_id(1)
    rows = tr_ref[i]

    @pl.when((rows > 0) & (tf_ref[i] == 1))
    def _():
        wgbf_ref[...] = wg_ref[...].astype(BF16)
        wubf_ref[...] = wu_ref[...].astype(BF16)

    def compute(n):
        if norm:
            x, rstd = _scaled_rows(x_ref[0:n, :], g_ref[...])
        else:
            x = x_ref[0:n, :]
        a = jnp.dot(x, wgbf_ref[...], preferred_element_type=F32)
        b = jnp.dot(x, wubf_ref[...], preferred_element_type=F32)
        if norm:
            a = rstd * a
            b = rstd * b
        o_ref[0:n, :] = (a * jax.nn.sigmoid(a) * b).astype(o_ref.dtype)

    _for_tile_rows(rows, x_ref.shape[0], parts, compute, o_ref)


def ffn_gate_up(x, w_gu, plan, bm, bn=512, gains=None, layer=0, parts=1):
    r, d = x.shape
    f = w_gu.shape[-1] // 2
    nfb = f // bn
    norm = gains is not None
    row = lambda j, i, te, tf, tr, tx, to, ta: (tx[i], 0)
    in_specs = [pl.BlockSpec((bm, d), row)]
    args = [x]
    if norm:
        in_specs.append(pl.BlockSpec((None, 1, d), lambda j, i, te, tf, tr, tx, to, ta: (layer, 0, 0)))
        args.append(gains)
    in_specs += [pl.BlockSpec((None, d, bn), lambda j, i, te, tf, tr, tx, to, ta: (ta[i], 0, j)),
                 pl.BlockSpec((None, d, bn), lambda j, i, te, tf, tr, tx, to, ta: (te[i], 0, nfb + j))]
    grid_spec = pltpu.PrefetchScalarGridSpec(
        num_scalar_prefetch=6,
        grid=(nfb, r // bm),
        in_specs=in_specs,
        out_specs=pl.BlockSpec((bm, bn), lambda j, i, te, tf, tr, tx, to, ta: (to[i], j)),
        scratch_shapes=[pltpu.VMEM((d, bn), BF16), pltpu.VMEM((d, bn), BF16)],
    )
    return pl.pallas_call(
        functools.partial(_ffn_gu_body, norm=norm, parts=parts),
        grid_spec=grid_spec,
        out_shape=jax.ShapeDtypeStruct((r, f), BF16),
        compiler_params=_params("arbitrary", "arbitrary"),
        name="ffn_gate_up",
    )(*plan, _weight_set_ahead(plan), *args, w_gu, w_gu)


def _ffn_dn_body(te_ref, tf_ref, tr_ref, tx_ref, to_ref, ta_ref, x_ref, wa_ref, wb_ref, *rest, residual, parts):
    if residual:
        h_ref, o_ref, wbf_ref = rest
    else:
        o_ref, wbf_ref = rest
    i = pl.program_id(1)
    rows = tr_ref[i]
    half = wa_ref.shape[0]

    @pl.when((rows > 0) & (tf_ref[i] == 1))
    def _():
        wbf_ref[0:half, :] = wa_ref[...].astype(BF16)
        wbf_ref[half:, :] = wb_ref[...].astype(BF16)

    def compute(n):
        y = jnp.dot(x_ref[0:n, :], wbf_ref[...], preferred_element_type=F32)
        if residual:
            y = h_ref[0:n, :] + y
        o_ref[0:n, :] = y

    _for_tile_rows(rows, x_ref.shape[0], parts, compute, o_ref)


def ffn_down(x, w_dn, plan, bm, h=None, bn=512, parts=1):
    r, f = x.shape
    d = w_dn.shape[-1]
    row = lambda j, i, te, tf, tr, tx, to, ta: (tx[i], 0)
    tile = lambda j, i, te, tf, tr, tx, to, ta: (to[i], j)
    in_specs = [pl.BlockSpec((bm, f), row),
                pl.BlockSpec((None, f // 2, bn), lambda j, i, te, tf, tr, tx, to, ta: (ta[i], 0, j)),
                pl.BlockSpec((None, f // 2, bn), lambda j, i, te, tf, tr, tx, to, ta: (te[i], 1, j))]
    args = [x, w_dn, w_dn]
    if h is not None:
        in_specs.append(pl.BlockSpec((bm, bn), tile))
        args.append(h)
    grid_spec = pltpu.PrefetchScalarGridSpec(
        num_scalar_prefetch=6,
        grid=(d // bn, r // bm),
        in_specs=in_specs,
        out_specs=pl.BlockSpec((bm, bn), tile),
        scratch_shapes=[pltpu.VMEM((f, bn), BF16)],
    )
    return pl.pallas_call(
        functools.partial(_ffn_dn_body, residual=h is not None, parts=parts),
        grid_spec=grid_spec,
        out_shape=jax.ShapeDtypeStruct((r, d), F32),
        compiler_params=_params("arbitrary", "arbitrary"),
        name="ffn_down",
    )(*plan, _weight_set_ahead(plan), *args)


def _router_body(h_ref, g_ref, w_ref, b_ref, xn_ref, meta_ref, cnt_ref, carry_ref, *, tm):
    @pl.when(pl.program_id(0) == 0)
    def _():
        carry_ref[...] = jnp.zeros(carry_ref.shape, F32)

    h = h_ref[...]
    xn = h * lax.rsqrt(jnp.mean(h * h, axis=-1, keepdims=True) + RMS_EPS) * g_ref[...]
    xn_ref[...] = xn.astype(xn_ref.dtype)
    lane = lax.broadcasted_iota(jnp.int32, (tm, LANES), 1)
    w = w_ref[...]
    x_hi, w_hi = xn.astype(BF16), w.astype(BF16)
    x_lo, w_lo = (xn - x_hi.astype(F32)).astype(BF16), (w - w_hi.astype(F32)).astype(BF16)
    logits = (jnp.dot(x_hi, w_hi, preferred_element_type=F32)
              + (jnp.dot(x_hi, w_lo, preferred_element_type=F32)
                 + jnp.dot(x_lo, w_hi, preferred_element_type=F32))) + b_ref[...]
    neg = jnp.float32(-jnp.inf)
    logits = jnp.where(lane < N_EXPERTS, logits, neg)
    m1 = jnp.max(logits, axis=-1, keepdims=True)
    i1 = jnp.min(jnp.where(logits == m1, lane, LANES), axis=-1, keepdims=True)
    rest = jnp.where(lane == i1, neg, logits)
    m2 = jnp.max(rest, axis=-1, keepdims=True)
    i2 = jnp.min(jnp.where(rest == m2, lane, LANES), axis=-1, keepdims=True)
    e2 = jnp.exp(m2 - m1)
    g1 = 1.0 / (1.0 + e2)
    g2 = e2 / (1.0 + e2)
    hot1 = lane == i1
    hot2 = lane == i2
    hot = (hot1 | hot2).astype(BF16)
    row = lax.broadcasted_iota(jnp.int32, (tm, tm), 0)
    col = lax.broadcasted_iota(jnp.int32, (tm, tm), 1)
    before = (col < row).astype(BF16)
    carry = carry_ref[0:1, :]
    rank = jnp.dot(before, hot, preferred_element_type=F32) + carry
    r1 = jnp.sum(jnp.where(hot1, rank, 0.0), axis=-1, keepdims=True)
    r2 = jnp.sum(jnp.where(hot2, rank, 0.0), axis=-1, keepdims=True)
    total = carry + jnp.sum(hot.astype(F32), axis=0, keepdims=True)
    carry_ref[...] = jnp.broadcast_to(total, carry_ref.shape)
    cnt_ref[...] = jnp.broadcast_to(total, cnt_ref.shape)
    meta = jnp.where(lane == 0, i1.astype(F32), 0.0)
    meta = jnp.where(lane == 1, i2.astype(F32), meta)
    meta = jnp.where(lane == 2, g1, meta)
    meta = jnp.where(lane == 3, g2, meta)
    meta = jnp.where(lane == 4, r1, meta)
    meta = jnp.where(lane == 5, r2, meta)
    meta_ref[...] = meta


def router(h, gains, layer, w_pad, b_pad, tm=512):
    t, d = h.shape
    return pl.pallas_call(
        functools.partial(_router_body, tm=tm),
        grid=(t // tm,),
        in_specs=[pl.BlockSpec((tm, d), lambda i: (i, 0)),
                  pl.BlockSpec((None, 1, d), lambda i: (layer, 0, 0)),
                  pl.BlockSpec((d, LANES), lambda i: (0, 0)),
                  pl.BlockSpec((1, LANES), lambda i: (0, 0))],
        out_specs=[pl.BlockSpec((tm, d), lambda i: (i, 0)),
                   pl.BlockSpec((tm, LANES), lambda i: (i, 0)),
                   pl.BlockSpec((SUBLANES, LANES), lambda i: (0, 0))],
        out_shape=[jax.ShapeDtypeStruct((t, d), BF16),
                   jax.ShapeDtypeStruct((t, LANES), F32),
                   jax.ShapeDtypeStruct((SUBLANES, LANES), F32)],
        scratch_shapes=[pltpu.VMEM((SUBLANES, LANES), F32)],
        compiler_params=_params("arbitrary"),
        name="router",
    )(h, gains, w_pad, b_pad)


def _dispatch_body(pos_ref, x_ref, init_ref, o_ref, sem, *, tm):
    del init_ref
    base = pl.program_id(0) * (TOP_K * tm)

    def row_copy(r, k):
        p = pos_ref[base + TOP_K * r + k]
        return pltpu.make_async_copy(x_ref.at[r], o_ref.at[p], sem)

    def start(r, carry):
        for k in range(TOP_K):
            row_copy(r, k).start()
        return carry

    def wait(r, carry):
        for k in range(TOP_K):
            row_copy(r, k).wait()
        return carry

    lax.fori_loop(0, tm, start, 0)
    lax.fori_loop(0, tm, wait, 0)


def dispatch(xn, pos, rows, tm=512):
    t, d = xn.shape
    slab = (d // LANES, LANES)
    grid_spec = pltpu.PrefetchScalarGridSpec(
        num_scalar_prefetch=1,
        grid=(t // tm,),
        in_specs=[pl.BlockSpec((tm,) + slab, lambda i, pos: (i, 0, 0)),
                  pl.BlockSpec(memory_space=pl.ANY)],
        out_specs=pl.BlockSpec(memory_space=pl.ANY),
        scratch_shapes=[pltpu.SemaphoreType.DMA(())],
    )
    out = pl.pallas_call(
        functools.partial(_dispatch_body, tm=tm),
        grid_spec=grid_spec,
        out_shape=jax.ShapeDtypeStruct((rows,) + slab, xn.dtype),
        input_output_aliases={2: 0},
        compiler_params=_params("arbitrary"),
        name="moe_dispatch",
    )(pos, xn.reshape((t,) + slab), jnp.zeros((rows,) + slab, xn.dtype))
    return out.reshape(rows, d)


def _combine_body(pos_ref, y_ref, meta_ref, h_ref, o_ref, buf_ref, sem, *, tm):
    base = pl.program_id(0) * (TOP_K * tm)

    def row_copy(r, k):
        p = pos_ref[base + TOP_K * r + k]
        return pltpu.make_async_copy(y_ref.at[pl.ds(p, 1)], buf_ref.at[k, pl.ds(r, 1)], sem)

    def start(r, carry):
        for k in range(TOP_K):
            row_copy(r, k).start()
        return carry

    def wait(r, carry):
        for k in range(TOP_K):
            row_copy(r, k).wait()
        return carry

    lax.fori_loop(0, tm, start, 0)
    lax.fori_loop(0, tm, wait, 0)
    meta = meta_ref[...]
    o_ref[...] = h_ref[...] + meta[:, 2:3] * buf_ref[0] + meta[:, 3:4] * buf_ref[1]


def combine(y, pos, meta, h, tm=256):
    t, d = h.shape
    grid_spec = pltpu.PrefetchScalarGridSpec(
        num_scalar_prefetch=1,
        grid=(t // tm,),
        in_specs=[pl.BlockSpec(memory_space=pl.ANY),
                  pl.BlockSpec((tm, LANES), lambda i, pos: (i, 0)),
                  pl.BlockSpec((tm, d), lambda i, pos: (i, 0))],
        out_specs=pl.BlockSpec((tm, d), lambda i, pos: (i, 0)),
        scratch_shapes=[pltpu.VMEM((TOP_K, tm, d), F32), pltpu.SemaphoreType.DMA(())],
    )
    return pl.pallas_call(
        functools.partial(_combine_body, tm=tm),
        grid_spec=grid_spec,
        out_shape=jax.ShapeDtypeStruct((t, d), F32),
        compiler_params=_params("arbitrary"),
        name="moe_combine",
    )(pos, y, meta, h)


def _group_plan(counts, group_start, n_rows, bm, first_expert):
    n_experts = counts.shape[0]
    n_tiles = n_rows // bm
    first_tile = group_start // bm
    tile_ids = jnp.arange(n_tiles, dtype=jnp.int32)
    group = (jnp.sum(tile_ids[:, None] >= first_tile[None, :], axis=1) - 1).astype(jnp.int32)
    rows = jnp.clip(counts[group] - (tile_ids - first_tile[group]) * bm, 0, bm).astype(jnp.int32)
    block = jnp.where(rows > 0, group, (group + 1) % n_experts)
    order = jnp.argsort(block * 3 + (rows > 0) + (rows == bm), stable=True).astype(jnp.int32)
    rows, block = rows[order], block[order]
    busy = rows > 0
    last_busy = lax.cummax(jnp.where(busy, tile_ids, -1))
    last_busy = jnp.where(last_busy < 0, jnp.argmax(busy).astype(jnp.int32), last_busy)
    prev_busy = jnp.concatenate([last_busy[:1], last_busy[:-1]])
    first = busy & ((jnp.cumsum(busy) == 1) | (block != block[prev_busy]))
    return (block + first_expert, first.astype(jnp.int32), rows, order[last_busy], order)


def moe_ffn(h, gains, layer, router_w, router_b, w_gu, w_dn, moe_layer, bm_up=1024, bm_down=512):
    t, d = h.shape
    e = N_EXPERTS
    w_pad = jnp.pad(router_w[moe_layer], ((0, 0), (0, LANES - e)))
    b_pad = jnp.pad(router_b[moe_layer], (0, LANES - e)).reshape(1, LANES)
    xn, meta, counts = router(h, gains, layer, w_pad, b_pad)
    counts = counts[0, :e].astype(jnp.int32)
    group_rows = ((counts + bm_up - 1) // bm_up) * bm_up
    group_start = jnp.cumsum(group_rows) - group_rows
    n_rows = ((TOP_K * t + e * (bm_up - 1)) // bm_up) * bm_up
    sel = meta[:, 0:TOP_K].astype(jnp.int32)
    rank = meta[:, 4:4 + TOP_K].astype(jnp.int32)
    pos = (group_start[sel] + rank).reshape(-1)
    x_sorted = dispatch(xn, pos, n_rows)
    mid = ffn_gate_up(x_sorted, w_gu, _group_plan(counts, group_start, n_rows, bm_up, moe_layer * e),
                      bm_up, parts=4)
    y = ffn_down(mid, w_dn, _group_plan(counts, group_start, n_rows, bm_down, moe_layer * e),
                 bm_down, parts=2)
    return combine(y, pos, meta, h)


def dense_ffn(h, gains, layer, w_gu, w_dn, dense_layer, bm_up=1024, bm_down=512):
    t = h.shape[0]

    def one_group(bm):
        n_tiles = t // bm
        tile_ids = jnp.arange(n_tiles, dtype=jnp.int32)
        return (jnp.full((n_tiles,), dense_layer, jnp.int32),
                (tile_ids == 0).astype(jnp.int32),
                jnp.full((n_tiles,), bm, jnp.int32),
                tile_ids, tile_ids)

    mid = ffn_gate_up(h, w_gu, one_group(bm_up), bm_up, gains=gains, layer=layer)
    return ffn_down(mid, w_dn, one_group(bm_down), bm_down, h=h)


def _ple_body(p_ref, x_ref, g_ref, wp_ref, wg_ref, h_ref, o_ref, wpbf_ref, wgbf_ref):
    @pl.when(pl.program_id(1) == 0)
    def _():
        wpbf_ref[...] = wp_ref[...].astype(BF16)
        wgbf_ref[...] = wg_ref[...].astype(BF16)

    e = jnp.dot(p_ref[...].astype(BF16), wpbf_ref[...], preferred_element_type=F32)
    xg, rstd = _scaled_rows(x_ref[...], g_ref[...])
    gate = jax.nn.sigmoid(rstd * jnp.dot(xg, wgbf_ref[...], preferred_element_type=F32))
    o_ref[...] = h_ref[...] + e * gate


def ple(h, gains, p, ple_w, gate_w, layer, bm=512, bn=1024):
    t, d = h.shape
    pd = p.shape[-1]
    return pl.pallas_call(
        _ple_body,
        grid=(d // bn, t // bm),
        in_specs=[pl.BlockSpec((None, bm, pd), lambda j, i: (layer, i, 0)),
                  pl.BlockSpec((bm, d), lambda j, i: (i, 0)),
                  pl.BlockSpec((None, 1, d), lambda j, i: (layer, 0, 0)),
                  pl.BlockSpec((None, pd, bn), lambda j, i: (layer, 0, j)),
                  pl.BlockSpec((None, d, bn), lambda j, i: (layer, 0, j)),
                  pl.BlockSpec((bm, bn), lambda j, i: (i, j))],
        out_specs=pl.BlockSpec((bm, bn), lambda j, i: (i, j)),
        out_shape=jax.ShapeDtypeStruct((t, d), F32),
        scratch_shapes=[pltpu.VMEM((pd, bn), BF16), pltpu.VMEM((d, bn), BF16)],
        compiler_params=_params("arbitrary", "arbitrary"),
        name="ple",
    )(p, h, gains, ple_w, gate_w, h)


def kernel(x, p, mix_norm, w_in, conv_w, conv_b, conv_ln_g, conv_ln_b, hg_lb_logits, hg_norm, w_out, ffn_norm, ffn_w_gu, ffn_w_dn, router_w, router_b, moe_w_gu, moe_w_dn, ple_w, ple_norm, ple_gate_w, final_norm):
    batch, seq, d = x.shape
    depth = w_in.shape[0]
    t = batch * seq
    d_conv = conv_w.shape[-1]
    heads = (w_out.shape[1] - d_conv) // HG_DV
    col_hg = 2 * d_conv

    h = x.reshape(t, d)
    p2 = p.reshape(depth, t, p.shape[-1])
    vec3 = lambda a: a.reshape(a.shape[0], 1, a.shape[-1])
    mix_g, ffn_g, ple_g = vec3(mix_norm), vec3(ffn_norm), vec3(ple_norm)
    cb, lng, lnb, gn = vec3(conv_b), vec3(conv_ln_g), vec3(conv_ln_b), vec3(hg_norm)
    moe_gu = moe_w_gu.reshape((-1,) + moe_w_gu.shape[2:])
    moe_dn = moe_w_dn.reshape((-1,) + moe_w_dn.shape[2:])

    for l in range(depth):
        z = proj_in(h, mix_g, w_in, l)
        yc = conv_group(z, conv_w, cb, lng, lnb, l, batch, seq, d_conv)
        yh = hgrn_group(z, hg_lb_logits, gn, l, batch, seq, heads, col_hg)
        h = proj_out(yc, yh, w_out, h, l)
        if l % 2 == 0:
            h = dense_ffn(h, ffn_g, l, ffn_w_gu, ffn_w_dn, l // 2)
        else:
            h = moe_ffn(h, ffn_g, l, router_w, router_b, moe_gu, moe_dn, l // 2)
        h = ple(h, ple_g, p2, ple_w, ple_gate_w, l)
    out = rmsnorm(h, final_norm.reshape(1, 1, d), 0, F32)
    return out.reshape(batch, seq, d)
```

```python
import functools

import jax
import jax.numpy as jnp
from jax import lax
from jax.experimental import pallas as pl
from jax.experimental.pallas import tpu as pltpu

F32 = jnp.float32
BF16 = jnp.bfloat16

RMS_EPS = 1e-6
LN_EPS = 1e-5
CONV_WIDTH = 31
HG_DK = 128
HG_DV = 128
CHUNK = 64
N_EXPERTS = 8
TOP_K = 2

LANES = 128
SUBLANES = 8
HGRN_UNROLL = 16
VMEM_LIMIT = 56 * 1024 * 1024

CONV_HALO = 32
CONV_STRIP = 32


def _params(*sem):
    return pltpu.CompilerParams(dimension_semantics=sem, vmem_limit_bytes=VMEM_LIMIT)


def _rmsnorm_body(x_ref, g_ref, o_ref):
    x = x_ref[...]
    ms = jnp.mean(x * x, axis=-1, keepdims=True)
    o_ref[...] = (x * lax.rsqrt(ms + RMS_EPS) * g_ref[...]).astype(o_ref.dtype)


def rmsnorm(h, gains, layer, out_dtype, bm=512):
    t, d = h.shape
    return pl.pallas_call(
        _rmsnorm_body,
        grid=(t // bm,),
        in_specs=[pl.BlockSpec((bm, d), lambda i: (i, 0)),
                  pl.BlockSpec((None, 1, d), lambda i: (layer, 0, 0))],
        out_specs=pl.BlockSpec((bm, d), lambda i: (i, 0)),
        out_shape=jax.ShapeDtypeStruct((t, d), out_dtype),
        compiler_params=_params("parallel"),
        name="rmsnorm",
    )(h, gains)


def _scaled_rows(x, gain):
    rstd = lax.rsqrt(jnp.mean(x * x, axis=-1, keepdims=True) + RMS_EPS)
    return (x * gain).astype(BF16), rstd


def _proj_in_body(x_ref, g_ref, w_ref, o_ref, wbf_ref):
    @pl.when(pl.program_id(1) == 0)
    def _():
        wbf_ref[...] = w_ref[...].astype(BF16)

    xg, rstd = _scaled_rows(x_ref[...], g_ref[...])
    o_ref[...] = rstd * jnp.dot(xg, wbf_ref[...], preferred_element_type=F32)


def proj_in(h, gains, w_in, layer, bm=512, bn=1536):
    t, d = h.shape
    n = w_in.shape[-1]
    return pl.pallas_call(
        _proj_in_body,
        grid=(n // bn, t // bm),
        in_specs=[pl.BlockSpec((bm, d), lambda j, i: (i, 0)),
                  pl.BlockSpec((None, 1, d), lambda j, i: (layer, 0, 0)),
                  pl.BlockSpec((None, d, bn), lambda j, i: (layer, 0, j))],
        out_specs=pl.BlockSpec((bm, bn), lambda j, i: (i, j)),
        out_shape=jax.ShapeDtypeStruct((t, n), F32),
        scratch_shapes=[pltpu.VMEM((d, bn), BF16)],
        compiler_params=_params("arbitrary", "arbitrary"),
        name="proj_in",
    )(h, gains, w_in)


def _conv_body(val_ref, gate_ref, w_ref, b_ref, lng_ref, lnb_ref, o_ref, u_ref, *, ts):
    s = pl.program_id(1)

    @pl.when(s == 0)
    def _():
        u_ref[0:CONV_HALO, :] = jnp.zeros((CONV_HALO, u_ref.shape[1]), F32)
        u_ref[CONV_HALO + ts:, :] = jnp.zeros((8, u_ref.shape[1]), F32)

    @pl.when(s > 0)
    def _():
        u_ref[0:CONV_HALO, :] = u_ref[ts:ts + CONV_HALO, :]

    u_ref[CONV_HALO:CONV_HALO + ts, :] = val_ref[...] * jax.nn.sigmoid(gate_ref[...])

    bias = b_ref[...]
    lng = lng_ref[...]
    lnb = lnb_ref[...]
    first_tap = CONV_HALO - (CONV_WIDTH - 1)

    n_win = CONV_STRIP + CONV_HALO + SUBLANES

    def strip(r, carry):
        r0 = pl.multiple_of(r * CONV_STRIP, CONV_STRIP)
        win = u_ref[pl.ds(r0, n_win), :]
        acc = jnp.zeros((CONV_STRIP, u_ref.shape[1]), F32)
        for sh in range(SUBLANES):
            shifted = win if sh == 0 else pltpu.roll(win, n_win - sh, axis=0)
            for a in range((CONV_HALO + SUBLANES) // SUBLANES):
                j = SUBLANES * a + sh - first_tap
                if 0 <= j < CONV_WIDTH:
                    acc = acc + w_ref[j:j + 1, :] * shifted[SUBLANES * a:SUBLANES * a + CONV_STRIP, :]
        c = acc + bias
        mu = jnp.mean(c, axis=-1, keepdims=True)
        cc = c - mu
        var = jnp.mean(cc * cc, axis=-1, keepdims=True)
        y = cc * lax.rsqrt(var + LN_EPS) * lng + lnb
        o_ref[pl.ds(r0, CONV_STRIP), :] = (y * jax.nn.sigmoid(y)).astype(o_ref.dtype)
        return carry

    lax.fori_loop(0, ts // CONV_STRIP, strip, 0)


def conv_group(z, conv_w, conv_b, ln_g, ln_b, layer, batch, seq, d_conv, ts=256):
    t = z.shape[0]
    nst = seq // ts
    vec = lambda: pl.BlockSpec((None, 1, d_conv), lambda b, s: (layer, 0, 0))
    return pl.pallas_call(
        functools.partial(_conv_body, ts=ts),
        grid=(batch, nst),
        in_specs=[pl.BlockSpec((ts, d_conv), lambda b, s: (b * nst + s, 0)),
                  pl.BlockSpec((ts, d_conv), lambda b, s: (b * nst + s, 1)),
                  pl.BlockSpec((None, CONV_WIDTH, d_conv), lambda b, s: (layer, 0, 0)),
                  vec(), vec(), vec()],
        out_specs=pl.BlockSpec((ts, d_conv), lambda b, s: (b * nst + s, 0)),
        out_shape=jax.ShapeDtypeStruct((t, d_conv), BF16),
        scratch_shapes=[pltpu.VMEM((CONV_HALO + ts + 8, d_conv), F32)],
        compiler_params=_params("arbitrary", "arbitrary"),
        name="conv_group",
    )(z, z, conv_w, conv_b, ln_g, ln_b)


_HALF_BLOCKS = (32, 16, 8, 4, 2, 1)


def _hgrn_body(lbl_ref, gn_ref, q_ref, f_ref, i_ref, g_ref, o_ref, b_scr, *, layer, n_chunks):
    lg = lbl_ref[...]
    ex = jnp.exp(lg - jnp.max(lg, axis=0, keepdims=True))
    sm = ex / jnp.sum(ex, axis=0, keepdims=True)
    lb = jnp.zeros((1, HG_DK), F32)
    for l in range(1, layer + 1):
        lb = lb + sm[l:l + 1, :]
    gn = gn_ref[...]

    c = CHUNK
    row = lax.broadcasted_iota(jnp.int32, (c, c), 0)
    col = lax.broadcasted_iota(jnp.int32, (c, c), 1)
    tril = (col <= row).astype(BF16)
    level = jnp.where(row == col, 0, -1)
    for lv, hb in enumerate(_HALF_BLOCKS):
        level = jnp.where((row // (2 * hb) == col // (2 * hb))
                          & (row % (2 * hb) >= hb) & (col % (2 * hb) < hb), lv + 1, level)
    sub = lax.broadcasted_iota(jnp.int32, (SUBLANES, HG_DK), 0)
    row_k = lax.broadcasted_iota(jnp.int32, (c, HG_DK), 0)
    nt = (((1,), (1,)), ((), ()))
    tn = (((0,), (0,)), ((), ()))

    def reference_rows(b, b_ref, hb):
        if hb == 1:
            return jnp.where(row_k % 2 == 0, b, pltpu.roll(b, 1, axis=0))
        bcast = lambda r: jnp.broadcast_to(b_ref[pl.ds(r, 1), :], (SUBLANES, HG_DK))
        groups = []
        for j in range(c // SUBLANES):
            r = SUBLANES * j
            if hb == 2:
                groups.append(jnp.where(sub < 4, bcast(r + 1), bcast(r + 5)))
            else:
                groups.append(bcast((r // (2 * hb)) * (2 * hb) + hb - 1))
        return jnp.concatenate(groups, axis=0)

    def chunk(ci, state_t, b_ref):
        r0 = pl.multiple_of(ci * c, c)
        q = q_ref[pl.ds(r0, c), :]
        f = lb + (1.0 - lb) * jax.nn.sigmoid(f_ref[pl.ds(r0, c), :])
        lf = jnp.log2(f)
        k = 1.0 - f
        v = i_ref[pl.ds(r0, c), :].astype(BF16)
        hi = lf.astype(BF16)
        rem = lf - hi.astype(F32)
        mid = rem.astype(BF16)
        lo = (rem - mid.astype(F32)).astype(BF16)
        b = (jnp.dot(tril, hi, preferred_element_type=F32)
             + jnp.dot(tril, mid, preferred_element_type=F32)
             + jnp.dot(tril, lo, preferred_element_type=F32))
        b_ref[...] = b
        scores = jnp.where(level == 0, lax.dot_general(q.astype(BF16), k.astype(BF16), nt,
                                                       preferred_element_type=F32), 0.0)
        for lv, hb in enumerate(_HALF_BLOCKS):
            e = jnp.exp2(-jnp.abs(b - reference_rows(b, b_ref, hb)))
            s_lv = lax.dot_general((q * e).astype(BF16), (k * e).astype(BF16), nt,
                                   preferred_element_type=F32)
            scores = jnp.where(level == lv + 1, s_lv, scores)
        o = (lax.dot_general((q * jnp.exp2(b)).astype(BF16), state_t.astype(BF16), nt,
                             preferred_element_type=F32)
             + jnp.dot(scores.astype(BF16), v, preferred_element_type=F32))
        b_last = b[c - 1:c, :]
        k_dec = (k * jnp.exp2(b_last - b)).astype(BF16)
        new_state_t = jnp.exp2(b_last) * state_t + lax.dot_general(
            v, k_dec, tn, preferred_element_type=F32)
        o = o * lax.rsqrt(jnp.mean(o * o, axis=-1, keepdims=True) + RMS_EPS) * gn
        g = g_ref[pl.ds(r0, c), :]
        o_ref[pl.ds(r0, c), :] = (o * (g * jax.nn.sigmoid(g))).astype(o_ref.dtype)
        return new_state_t

    def chunk_group(gi, state_t):
        for u in range(HGRN_UNROLL):
            state_t = chunk(gi * HGRN_UNROLL + u, state_t, b_scr.at[u])
        return state_t

    lax.fori_loop(0, n_chunks // HGRN_UNROLL, chunk_group, jnp.zeros((HG_DV, HG_DK), F32))


def hgrn_group(z, lb_logits, hg_norm, layer, batch, seq, heads, col0):
    t = z.shape[0]
    depth = lb_logits.shape[0]
    assert seq % (CHUNK * HGRN_UNROLL) == 0, "sequence must split into whole groups of unrolled chunks"
    c0 = col0 // LANES
    zspec = lambda off: pl.BlockSpec((seq, LANES), lambda b, h: (b, c0 + off * heads + h))
    return pl.pallas_call(
        functools.partial(_hgrn_body, layer=layer, n_chunks=seq // CHUNK),
        grid=(batch, heads),
        in_specs=[pl.BlockSpec((depth, HG_DK), lambda b, h: (0, h)),
                  pl.BlockSpec((None, 1, HG_DV), lambda b, h: (layer, 0, 0)),
                  zspec(0), zspec(1), zspec(2), zspec(3)],
        out_specs=pl.BlockSpec((seq, HG_DV), lambda b, h: (b, h)),
        out_shape=jax.ShapeDtypeStruct((t, heads * HG_DV), BF16),
        scratch_shapes=[pltpu.VMEM((HGRN_UNROLL, CHUNK, HG_DK), F32)],
        compiler_params=_params("parallel", "parallel"),
        name="hgrn_group",
    )(lb_logits, hg_norm, z, z, z, z)


def _proj_out_body(yc_ref, yh_ref, w_ref, h_ref, o_ref, wbf_ref, *, d_conv):
    @pl.when(pl.program_id(1) == 0)
    def _():
        wbf_ref[...] = w_ref[...].astype(BF16)

    y = (jnp.dot(yc_ref[...], wbf_ref[0:d_conv, :], preferred_element_type=F32)
         + jnp.dot(yh_ref[...], wbf_ref[d_conv:, :], preferred_element_type=F32))
    o_ref[...] = h_ref[...] + y


def proj_out(yc, yh, w_out, h, layer, bm=1024, bn=1024):
    t, d_conv = yc.shape
    d_hg = yh.shape[1]
    d = h.shape[1]
    return pl.pallas_call(
        functools.partial(_proj_out_body, d_conv=d_conv),
        grid=(d // bn, t // bm),
        in_specs=[pl.BlockSpec((bm, d_conv), lambda j, i: (i, 0)),
                  pl.BlockSpec((bm, d_hg), lambda j, i: (i, 0)),
                  pl.BlockSpec((None, d_conv + d_hg, bn), lambda j, i: (layer, 0, j)),
                  pl.BlockSpec((bm, bn), lambda j, i: (i, j))],
        out_specs=pl.BlockSpec((bm, bn), lambda j, i: (i, j)),
        out_shape=jax.ShapeDtypeStruct((t, d), F32),
        scratch_shapes=[pltpu.VMEM((d_conv + d_hg, bn), BF16)],
        compiler_params=_params("arbitrary", "arbitrary"),
        name="proj_out",
    )(yc, yh, w_out, h)


def _weight_set_ahead(plan):
    te, tf = plan[0], plan[1]
    return jnp.where(tf == 1, te, jnp.concatenate([te[1:], te[-1:]]))


def _for_tile_rows(rows, bm, parts, compute, o_ref):
    step = bm // parts
    for part in range(1, parts + 1):
        n = part * step

        @pl.when((rows > n - step) & (rows <= n))
        def _(n=n):
            compute(n)
            if n < bm:
                o_ref[n:, :] = jnp.zeros((bm - n, o_ref.shape[1]), o_ref.dtype)

    @pl.when(rows == 0)
    def _():
        o_ref[...] = jnp.zeros(o_ref.shape, o_ref.dtype)


def _ffn_gu_body(te_ref, tf_ref, tr_ref, tx_ref, to_ref, ta_ref, x_ref, *rest, norm, parts):
    if norm:
        g_ref, wg_ref, wu_ref, o_ref, wgbf_ref, wubf_ref = rest
    else:
        wg_ref, wu_ref, o_ref, wgbf_ref, wubf_ref = rest
    i = pl.program_id(1)
    rows = tr_ref[i]

    @pl.when((rows > 0) & (tf_ref[i] == 1))
    def _():
        wgbf_ref[...] = wg_ref[...].astype(BF16)
        wubf_ref[...] = wu_ref[...].astype(BF16)

    def compute(n):
        if norm:
            x, rstd = _scaled_rows(x_ref[0:n, :], g_ref[...])
        else:
            x = x_ref[0:n, :]
        a = jnp.dot(x, wgbf_ref[...], preferred_element_type=F32)
        b = jnp.dot(x, wubf_ref[...], preferred_element_type=F32)
        if norm:
            a = rstd * a
            b = rstd * b
        o_ref[0:n, :] = (a * jax.nn.sigmoid(a) * b).astype(o_ref.dtype)

    _for_tile_rows(rows, x_ref.shape[0], parts, compute, o_ref)


def ffn_gate_up(x, w_gu, plan, bm, bn=512, gains=None, layer=0, parts=1):
    r, d = x.shape
    f = w_gu.shape[-1] // 2
    nfb = f // bn
    norm = gains is not None
    row = lambda j, i, te, tf, tr, tx, to, ta: (tx[i], 0)
    in_specs = [pl.BlockSpec((bm, d), row)]
    args = [x]
    if norm:
        in_specs.append(pl.BlockSpec((None, 1, d), lambda j, i, te, tf, tr, tx, to, ta: (layer, 0, 0)))
        args.append(gains)
    in_specs += [pl.BlockSpec((None, d, bn), lambda j, i, te, tf, tr, tx, to, ta: (ta[i], 0, j)),
                 pl.BlockSpec((None, d, bn), lambda j, i, te, tf, tr, tx, to, ta: (te[i], 0, nfb + j))]
    grid_spec = pltpu.PrefetchScalarGridSpec(
        num_scalar_prefetch=6,
        grid=(nfb, r // bm),
        in_specs=in_specs,
        out_specs=pl.BlockSpec((bm, bn), lambda j, i, te, tf, tr, tx, to, ta: (to[i], j)),
        scratch_shapes=[pltpu.VMEM((d, bn), BF16), pltpu.VMEM((d, bn), BF16)],
    )
    return pl.pallas_call(
        functools.partial(_ffn_gu_body, norm=norm, parts=parts),
        grid_spec=grid_spec,
        out_shape=jax.ShapeDtypeStruct((r, f), BF16),
        compiler_params=_params("arbitrary", "arbitrary"),
        name="ffn_gate_up",
    )(*plan, _weight_set_ahead(plan), *args, w_gu, w_gu)


def _ffn_dn_body(te_ref, tf_ref, tr_ref, tx_ref, to_ref, ta_ref, x_ref, wa_ref, wb_ref, *rest, residual, parts):
    if residual:
        h_ref, o_ref, wbf_ref = rest
    else:
        o_ref, wbf_ref = rest
    i = pl.program_id(1)
    rows = tr_ref[i]
    half = wa_ref.shape[0]

    @pl.when((rows > 0) & (tf_ref[i] == 1))
    def _():
        wbf_ref[0:half, :] = wa_ref[...].astype(BF16)
        wbf_ref[half:, :] = wb_ref[...].astype(BF16)

    def compute(n):
        y = jnp.dot(x_ref[0:n, :], wbf_ref[...], preferred_element_type=F32)
        if residual:
            y = h_ref[0:n, :] + y
        o_ref[0:n, :] = y

    _for_tile_rows(rows, x_ref.shape[0], parts, compute, o_ref)


def ffn_down(x, w_dn, plan, bm, h=None, bn=512, parts=1):
    r, f = x.shape
    d = w_dn.shape[-1]
    row = lambda j, i, te, tf, tr, tx, to, ta: (tx[i], 0)
    tile = lambda j, i, te, tf, tr, tx, to, ta: (to[i], j)
    in_specs = [pl.BlockSpec((bm, f), row),
                pl.BlockSpec((None, f // 2, bn), lambda j, i, te, tf, tr, tx, to, ta: (ta[i], 0, j)),
                pl.BlockSpec((None, f // 2, bn), lambda j, i, te, tf, tr, tx, to, ta: (te[i], 1, j))]
    args = [x, w_dn, w_dn]
    if h is not None:
        in_specs.append(pl.BlockSpec((bm, bn), tile))
        args.append(h)
    grid_spec = pltpu.PrefetchScalarGridSpec(
        num_scalar_prefetch=6,
        grid=(d // bn, r // bm),
        in_specs=in_specs,
        out_specs=pl.BlockSpec((bm, bn), tile),
        scratch_shapes=[pltpu.VMEM((f, bn), BF16)],
    )
    return pl.pallas_call(
        functools.partial(_ffn_dn_body, residual=h is not None, parts=parts),
        grid_spec=grid_spec,
        out_shape=jax.ShapeDtypeStruct((r, d), F32),
        compiler_params=_params("arbitrary", "arbitrary"),
        name="ffn_down",
    )(*plan, _weight_set_ahead(plan), *args)


def _router_body(h_ref, g_ref, w_ref, b_ref, xn_ref, meta_ref, cnt_ref, carry_ref, *, tm):
    @pl.when(pl.program_id(0) == 0)
    def _():
        carry_ref[...] = jnp.zeros(carry_ref.shape, F32)

    h = h_ref[...]
    xn = h * lax.rsqrt(jnp.mean(h * h, axis=-1, keepdims=True) + RMS_EPS) * g_ref[...]
    xn_ref[...] = xn.astype(xn_ref.dtype)
    lane = lax.broadcasted_iota(jnp.int32, (tm, LANES), 1)
    w = w_ref[...]
    x_hi, w_hi = xn.astype(BF16), w.astype(BF16)
    x_lo, w_lo = (xn - x_hi.astype(F32)).astype(BF16), (w - w_hi.astype(F32)).astype(BF16)
    logits = (jnp.dot(x_hi, w_hi, preferred_element_type=F32)
              + (jnp.dot(x_hi, w_lo, preferred_element_type=F32)
                 + jnp.dot(x_lo, w_hi, preferred_element_type=F32))) + b_ref[...]
    neg = jnp.float32(-jnp.inf)
    logits = jnp.where(lane < N_EXPERTS, logits, neg)
    m1 = jnp.max(logits, axis=-1, keepdims=True)
    i1 = jnp.min(jnp.where(logits == m1, lane, LANES), axis=-1, keepdims=True)
    rest = jnp.where(lane == i1, neg, logits)
    m2 = jnp.max(rest, axis=-1, keepdims=True)
    i2 = jnp.min(jnp.where(rest == m2, lane, LANES), axis=-1, keepdims=True)
    e2 = jnp.exp(m2 - m1)
    g1 = 1.0 / (1.0 + e2)
    g2 = e2 / (1.0 + e2)
    hot1 = lane == i1
    hot2 = lane == i2
    hot = (hot1 | hot2).astype(BF16)
    row = lax.broadcasted_iota(jnp.int32, (tm, tm), 0)
    col = lax.broadcasted_iota(jnp.int32, (tm, tm), 1)
    before = (col < row).astype(BF16)
    carry = carry_ref[0:1, :]
    rank = jnp.dot(before, hot, preferred_element_type=F32) + carry
    r1 = jnp.sum(jnp.where(hot1, rank, 0.0), axis=-1, keepdims=True)
    r2 = jnp.sum(jnp.where(hot2, rank, 0.0), axis=-1, keepdims=True)
    total = carry + jnp.sum(hot.astype(F32), axis=0, keepdims=True)
    carry_ref[...] = jnp.broadcast_to(total, carry_ref.shape)
    cnt_ref[...] = jnp.broadcast_to(total, cnt_ref.shape)
    meta = jnp.where(lane == 0, i1.astype(F32), 0.0)
    meta = jnp.where(lane == 1, i2.astype(F32), meta)
    meta = jnp.where(lane == 2, g1, meta)
    meta = jnp.where(lane == 3, g2, meta)
    meta = jnp.where(lane == 4, r1, meta)
    meta = jnp.where(lane == 5, r2, meta)
    meta_ref[...] = meta


def router(h, gains, layer, w_pad, b_pad, tm=512):
    t, d = h.shape
    return pl.pallas_call(
        functools.partial(_router_body, tm=tm),
        grid=(t // tm,),
        in_specs=[pl.BlockSpec((tm, d), lambda i: (i, 0)),
                  pl.BlockSpec((None, 1, d), lambda i: (layer, 0, 0)),
                  pl.BlockSpec((d, LANES), lambda i: (0, 0)),
                  pl.BlockSpec((1, LANES), lambda i: (0, 0))],
        out_specs=[pl.BlockSpec((tm, d), lambda i: (i, 0)),
                   pl.BlockSpec((tm, LANES), lambda i: (i, 0)),
                   pl.BlockSpec((SUBLANES, LANES), lambda i: (0, 0))],
        out_shape=[jax.ShapeDtypeStruct((t, d), BF16),
                   jax.ShapeDtypeStruct((t, LANES), F32),
                   jax.ShapeDtypeStruct((SUBLANES, LANES), F32)],
        scratch_shapes=[pltpu.VMEM((SUBLANES, LANES), F32)],
        compiler_params=_params("arbitrary"),
        name="router",
    )(h, gains, w_pad, b_pad)


def _dispatch_body(pos_ref, x_ref, init_ref, o_ref, sem, *, tm):
    del init_ref
    base = pl.program_id(0) * (TOP_K * tm)

    def row_copy(r, k):
        p = pos_ref[base + TOP_K * r + k]
        return pltpu.make_async_copy(x_ref.at[r], o_ref.at[p], sem)

    def start(r, carry):
        for k in range(TOP_K):
            row_copy(r, k).start()
        return carry

    def wait(r, carry):
        for k in range(TOP_K):
            row_copy(r, k).wait()
        return carry

    lax.fori_loop(0, tm, start, 0)
    lax.fori_loop(0, tm, wait, 0)


def dispatch(xn, pos, rows, tm=512):
    t, d = xn.shape
    slab = (d // LANES, LANES)
    grid_spec = pltpu.PrefetchScalarGridSpec(
        num_scalar_prefetch=1,
        grid=(t // tm,),
        in_specs=[pl.BlockSpec((tm,) + slab, lambda i, pos: (i, 0, 0)),
                  pl.BlockSpec(memory_space=pl.ANY)],
        out_specs=pl.BlockSpec(memory_space=pl.ANY),
        scratch_shapes=[pltpu.SemaphoreType.DMA(())],
    )
    out = pl.pallas_call(
        functools.partial(_dispatch_body, tm=tm),
        grid_spec=grid_spec,
        out_shape=jax.ShapeDtypeStruct((rows,) + slab, xn.dtype),
        input_output_aliases={2: 0},
        compiler_params=_params("arbitrary"),
        name="moe_dispatch",
    )(pos, xn.reshape((t,) + slab), jnp.zeros((rows,) + slab, xn.dtype))
    return out.reshape(rows, d)


def _combine_body(pos_ref, y_ref, meta_ref, h_ref, o_ref, buf_ref, sem, *, tm):
    base = pl.program_id(0) * (TOP_K * tm)

    def row_copy(r, k):
        p = pos_ref[base + TOP_K * r + k]
        return pltpu.make_async_copy(y_ref.at[pl.ds(p, 1)], buf_ref.at[k, pl.ds(r, 1)], sem)

    def start(r, carry):
        for k in range(TOP_K):
            row_copy(r, k).start()
        return carry

    def wait(r, carry):
        for k in range(TOP_K):
            row_copy(r, k).wait()
        return carry

    lax.fori_loop(0, tm, start, 0)
    lax.fori_loop(0, tm, wait, 0)
    meta = meta_ref[...]
    o_ref[...] = h_ref[...] + meta[:, 2:3] * buf_ref[0] + meta[:, 3:4] * buf_ref[1]


def combine(y, pos, meta, h, tm=256):
    t, d = h.shape
    grid_spec = pltpu.PrefetchScalarGridSpec(
        num_scalar_prefetch=1,
        grid=(t // tm,),
        in_specs=[pl.BlockSpec(memory_space=pl.ANY),
                  pl.BlockSpec((tm, LANES), lambda i, pos: (i, 0)),
                  pl.BlockSpec((tm, d), lambda i, pos: (i, 0))],
        out_specs=pl.BlockSpec((tm, d), lambda i, pos: (i, 0)),
        scratch_shapes=[pltpu.VMEM((TOP_K, tm, d), F32), pltpu.SemaphoreType.DMA(())],
    )
    return pl.pallas_call(
        functools.partial(_combine_body, tm=tm),
        grid_spec=grid_spec,
        out_shape=jax.ShapeDtypeStruct((t, d), F32),
        compiler_params=_params("arbitrary"),
        name="moe_combine",
    )(pos, y, meta, h)


def _group_plan(counts, group_start, n_rows, bm, first_expert):
    n_experts = counts.shape[0]
    n_tiles = n_rows // bm
    first_tile = group_start // bm
    tile_ids = jnp.arange(n_tiles, dtype=jnp.int32)
    group = (jnp.sum(tile_ids[:, None] >= first_tile[None, :], axis=1) - 1).astype(jnp.int32)
    rows = jnp.clip(counts[group] - (tile_ids - first_tile[group]) * bm, 0, bm).astype(jnp.int32)
    block = jnp.where(rows > 0, group, (group + 1) % n_experts)
    order = jnp.argsort(block * 3 + (rows > 0) + (rows == bm), stable=True).astype(jnp.int32)
    rows, block = rows[order], block[order]
    busy = rows > 0
    last_busy = lax.cummax(jnp.where(busy, tile_ids, -1))
    last_busy = jnp.where(last_busy < 0, jnp.argmax(busy).astype(jnp.int32), last_busy)
    prev_busy = jnp.concatenate([last_busy[:1], last_busy[:-1]])
    first = busy & ((jnp.cumsum(busy) == 1) | (block != block[prev_busy]))
    return (block + first_expert, first.astype(jnp.int32), rows, order[last_busy], order)


def moe_ffn(h, gains, layer, router_w, router_b, w_gu, w_dn, moe_layer, bm_up=1024, bm_down=512):
    t, d = h.shape
    e = N_EXPERTS
    w_pad = jnp.pad(router_w[moe_layer], ((0, 0), (0, LANES - e)))
    b_pad = jnp.pad(router_b[moe_layer], (0, LANES - e)).reshape(1, LANES)
    xn, meta, counts = router(h, gains, layer, w_pad, b_pad)
    counts = counts[0, :e].astype(jnp.int32)
    group_rows = ((counts + bm_up - 1) // bm_up) * bm_up
    group_start = jnp.cumsum(group_rows) - group_rows
    n_rows = ((TOP_K * t + e * (bm_up - 1)) // bm_up) * bm_up
    sel = meta[:, 0:TOP_K].astype(jnp.int32)
    rank = meta[:, 4:4 + TOP_K].astype(jnp.int32)
    pos = (group_start[sel] + rank).reshape(-1)
    x_sorted = dispatch(xn, pos, n_rows)
    mid = ffn_gate_up(x_sorted, w_gu, _group_plan(counts, group_start, n_rows, bm_up, moe_layer * e),
                      bm_up, parts=4)
    y = ffn_down(mid, w_dn, _group_plan(counts, group_start, n_rows, bm_down, moe_layer * e),
                 bm_down, parts=2)
    return combine(y, pos, meta, h)


def dense_ffn(h, gains, layer, w_gu, w_dn, dense_layer, bm_up=1024, bm_down=512):
    t = h.shape[0]

    def one_group(bm):
        n_tiles = t // bm
        tile_ids = jnp.arange(n_tiles, dtype=jnp.int32)
        return (jnp.full((n_tiles,), dense_layer, jnp.int32),
                (tile_ids == 0).astype(jnp.int32),
                jnp.full((n_tiles,), bm, jnp.int32),
                tile_ids, tile_ids)

    mid = ffn_gate_up(h, w_gu, one_group(bm_up), bm_up, gains=gains, layer=layer)
    return ffn_down(mid, w_dn, one_group(bm_down), bm_down, h=h)


def _ple_body(p_ref, x_ref, g_ref, wp_ref, wg_ref, h_ref, o_ref, wpbf_ref, wgbf_ref):
    @pl.when(pl.program_id(1) == 0)
    def _():
        wpbf_ref[...] = wp_ref[...].astype(BF16)
        wgbf_ref[...] = wg_ref[...].astype(BF16)

    e = jnp.dot(p_ref[...].astype(BF16), wpbf_ref[...], preferred_element_type=F32)
    xg, rstd = _scaled_rows(x_ref[...], g_ref[...])
    gate = jax.nn.sigmoid(rstd * jnp.dot(xg, wgbf_ref[...], preferred_element_type=F32))
    o_ref[...] = h_ref[...] + e * gate


def ple(h, gains, p, ple_w, gate_w, layer, bm=512, bn=1024):
    t, d = h.shape
    pd = p.shape[-1]
    return pl.pallas_call(
        _ple_body,
        grid=(d // bn, t // bm),
        in_specs=[pl.BlockSpec((None, bm, pd), lambda j, i: (layer, i, 0)),
                  pl.BlockSpec((bm, d), lambda j, i: (i, 0)),
                  pl.BlockSpec((None, 1, d), lambda j, i: (layer, 0, 0)),
                  pl.BlockSpec((None, pd, bn), lambda j, i: (layer, 0, j)),
                  pl.BlockSpec((None, d, bn), lambda j, i: (layer, 0, j)),
                  pl.BlockSpec((bm, bn), lambda j, i: (i, j))],
        out_specs=pl.BlockSpec((bm, bn), lambda j, i: (i, j)),
        out_shape=jax.ShapeDtypeStruct((t, d), F32),
        scratch_shapes=[pltpu.VMEM((pd, bn), BF16), pltpu.VMEM((d, bn), BF16)],
        compiler_params=_params("arbitrary", "arbitrary"),
        name="ple",
    )(p, h, gains, ple_w, gate_w, h)


def kernel(x, p, mix_norm, w_in, conv_w, conv_b, conv_ln_g, conv_ln_b, hg_lb_logits, hg_norm, w_out, ffn_norm, ffn_w_gu, ffn_w_dn, router_w, router_b, moe_w_gu, moe_w_dn, ple_w, ple_norm, ple_gate_w, final_norm):
    batch, seq, d = x.shape
    depth = w_in.shape[0]
    t = batch * seq
    d_conv = conv_w.shape[-1]
    heads = (w_out.shape[1] - d_conv) // HG_DV
    col_hg = 2 * d_conv

    h = x.reshape(t, d)
    p2 = p.reshape(depth, t, p.shape[-1])
    vec3 = lambda a: a.reshape(a.shape[0], 1, a.shape[-1])
    mix_g, ffn_g, ple_g = vec3(mix_norm), vec3(ffn_norm), vec3(ple_norm)
    cb, lng, lnb, gn = vec3(conv_b), vec3(conv_ln_g), vec3(conv_ln_b), vec3(hg_norm)
    moe_gu = moe_w_gu.reshape((-1,) + moe_w_gu.shape[2:])
    moe_dn = moe_w_dn.reshape((-1,) + moe_w_dn.shape[2:])

    for l in range(depth):
        z = proj_in(h, mix_g, w_in, l)
        yc = conv_group(z, conv_w, cb, lng, lnb, l, batch, seq, d_conv)
        yh = hgrn_group(z, hg_lb_logits, gn, l, batch, seq, heads, col_hg)
        h = proj_out(yc, yh, w_out, h, l)
        if l % 2 == 0:
            h = dense_ffn(h, ffn_g, l, ffn_w_gu, ffn_w_dn, l // 2)
        else:
            h = moe_ffn(h, ffn_g, l, router_w, router_b, moe_gu, moe_dn, l // 2)
        h = ple(h, ple_g, p2, ple_w, ple_gate_w, l)
    out = rmsnorm(h, final_norm.reshape(1, 1, d), 0, F32)
    return out.reshape(batch, seq, d)
```
